```python
import math
import jax, jax.numpy as jnp
from jax import lax
import numpy as np

D_MODEL = 2048
BATCH = 4
SEQ = 2048
DEPTH = 4
DEC_BATCH = 2
DEC_SEQ = 4096
PAST_LEN = 128

HEAD_DIM = 128
ATT_HEADS = (D_MODEL // 2) // HEAD_DIM
ATT_WIDTH = ATT_HEADS * HEAD_DIM
ROPE_DIM = HEAD_DIM // 4
ROPE_THETA = 500000.0
DILATED_BRANCHES = ((128, 1), (512, 4), (2048, 16))
SSM_HEAD_DIM = 64
SSM_WIDTH = D_MODEL // 2
SSM_HEADS = SSM_WIDTH // SSM_HEAD_DIM
SSM_GROUPS = 2
SSM_HEADS_PER_GROUP = SSM_HEADS // SSM_GROUPS
SSM_STATE = 128
SSM_CONV = 4
SSM_CHUNK = 128
SSM_CONV_CH = SSM_WIDTH + 2 * SSM_GROUPS * SSM_STATE
IN_COLS = 3 * ATT_WIDTH + SSM_WIDTH + SSM_CONV_CH + 2 * SSM_HEADS
CONF_CH = D_MODEL
CONF_KERNEL = 31
FFN_HIDDEN = -(-8 * D_MODEL // (3 * 256)) * 256
N_EVEN = (DEPTH + 1) // 2
N_ODD = DEPTH // 2
EPS = 1e-6
NEG_INF = -1e30

kernel_name = 'hybrid_dilated_ssd_conformer_encoder'


def rmsnorm(x, g):
    xf = x.astype(jnp.float32)
    y = xf * lax.rsqrt(jnp.mean(xf * xf, axis=-1, keepdims=True) + EPS)
    return (y * g.astype(jnp.float32)).astype(x.dtype)


def layernorm(x, g, b):
    xf = x.astype(jnp.float32)
    mu = jnp.mean(xf, axis=-1, keepdims=True)
    xc = xf - mu
    y = xc * lax.rsqrt(jnp.mean(xc * xc, axis=-1, keepdims=True) + EPS)
    return (y * g.astype(jnp.float32) + b.astype(jnp.float32)).astype(x.dtype)


def depthwise_conv(x, w, b):
    width = w.shape[0]
    left = (width - 1) // 2
    y = lax.conv_general_dilated(x, w[:, None, :].astype(x.dtype), window_strides=(1,),
                                 padding=[(left, width - 1 - left)],
                                 dimension_numbers=('NWC', 'WIO', 'NWC'),
                                 feature_group_count=x.shape[-1])
    return y + b.astype(y.dtype)


def partial_rope(t, cos, sin):
    rot, rest = t[..., :ROPE_DIM], t[..., ROPE_DIM:]
    r1, r2 = rot[..., :ROPE_DIM // 2], rot[..., ROPE_DIM // 2:]
    rot = jnp.concatenate([r1 * cos - r2 * sin, r2 * cos + r1 * sin], axis=-1)
    return jnp.concatenate([rot.astype(t.dtype), rest], axis=-1)


def to_residue_classes(t, dil):
    b, s = t.shape[:2]
    return t.reshape(b, s // dil, dil, *t.shape[2:]).swapaxes(1, 2).reshape(b * dil, s // dil, *t.shape[2:])


def from_residue_classes(t, batch, dil):
    l = t.shape[1]
    return t.reshape(batch, dil, l, *t.shape[2:]).swapaxes(1, 2).reshape(batch, dil * l, *t.shape[2:])


def dilated_branch(q, k, v, window, dil):
    batch = q.shape[0]
    half = window // (2 * dil)
    blk = half
    qc, kc, vc = (to_residue_classes(t, dil) for t in (q, k, v))
    bb, L = qc.shape[0], qc.shape[1]
    nb = -(-L // blk)
    lp = nb * blk
    qb = jnp.pad(qc, ((0, 0), (0, lp - L), (0, 0), (0, 0))).reshape(bb, nb, blk, ATT_HEADS, HEAD_DIM)

    def windows(t):
        tp = jnp.pad(t, ((0, 0), (blk, blk + lp - L), (0, 0), (0, 0))).reshape(bb, nb + 2, blk, ATT_HEADS, HEAD_DIM)
        return jnp.concatenate([tp[:, :-2], tp[:, 1:-1], tp[:, 2:]], axis=2)

    kw, vw = windows(kc), windows(vc)
    s = jnp.einsum('bnqhd,bnkhd->bnhqk', qb.astype(jnp.float32), kw.astype(jnp.float32)) * (HEAD_DIM ** -0.5)
    n = jnp.arange(nb)[:, None, None]
    qpos = n * blk + jnp.arange(blk)[None, :, None]
    kpos = (n - 1) * blk + jnp.arange(3 * blk)[None, None, :]
    allowed = (jnp.abs(qpos - kpos) <= half) & (kpos >= 0) & (kpos < L)
    s = jnp.where(allowed[None, :, None], s, NEG_INF)
    m = jnp.max(s, axis=-1)
    p = jnp.exp(s - m[..., None])
    l = jnp.sum(p, axis=-1)
    o = jnp.einsum('bnhqk,bnkhd->bnqhd', p, vw.astype(jnp.float32))

    def back(t):
        t = t.reshape(bb, lp, *t.shape[3:])[:, :L]
        return from_residue_classes(t, batch, dil)

    return back(o), back(m.swapaxes(2, 3)), back(l.swapaxes(2, 3))


def dilated_attention(q, k, v, q_norm, k_norm):
    bn, S, _ = q.shape
    q = rmsnorm(q.reshape(bn, S, ATT_HEADS, HEAD_DIM), q_norm)
    k = rmsnorm(k.reshape(bn, S, ATT_HEADS, HEAD_DIM), k_norm)
    v = v.reshape(bn, S, ATT_HEADS, HEAD_DIM)
    pos = jnp.arange(S, dtype=jnp.float32)
    inv_freq = ROPE_THETA ** (-jnp.arange(0, ROPE_DIM, 2, dtype=jnp.float32) / ROPE_DIM)
    ang = pos[:, None] * inv_freq[None, :]
    cos, sin = jnp.cos(ang)[:, None, :], jnp.sin(ang)[:, None, :]
    q = partial_rope(q, cos, sin)
    k = partial_rope(k, cos, sin)
    outs, maxes, dens = zip(*[dilated_branch(q, k, v, w, d) for (w, d) in DILATED_BRANCHES])
    maxes = jnp.stack(maxes)
    wts = jnp.exp(maxes - jnp.max(maxes, axis=0))
    num = jnp.einsum('gbsh,gbshd->bshd', wts, jnp.stack(outs))
    den = jnp.sum(wts * jnp.stack(dens), axis=0)
    out = num / den[..., None]
    return out.reshape(bn, S, ATT_WIDTH).astype(q.dtype)


def ssd_chunked(x, delta, a, bm, cm):
    bn, S, G, J, P = x.shape
    N = bm.shape[-1]
    T = SSM_CHUNK
    c = S // T
    X = (x * delta[..., None]).reshape(bn, c, T, G, J, P)
    adt = (delta * a).reshape(bn, c, T, G, J).transpose(0, 1, 3, 4, 2)
    Bc = bm.reshape(bn, c, T, G, N)
    Cc = cm.reshape(bn, c, T, G, N)
    a_cum = jnp.cumsum(adt, axis=-1)
    seg = a_cum[..., :, None] - a_cum[..., None, :]
    lower = jnp.tril(jnp.ones((T, T), dtype=bool))
    lmat = jnp.exp(jnp.where(lower, seg, -jnp.inf))
    cb = jnp.einsum('bclgn,bcsgn->bcgls', Cc, Bc)
    y_diag = jnp.einsum('bcgjls,bcsgjp->bclgjp', cb[:, :, :, None] * lmat, X)
    decay_states = jnp.exp(a_cum[..., -1:] - a_cum).transpose(0, 1, 4, 2, 3)
    states = jnp.einsum('bcsgn,bcsgjp->bcgjpn', Bc, X * decay_states[..., None])
    chunk_decay = jnp.exp(a_cum[..., -1])

    def step(h, inp):
        st, dec = inp
        return h * dec[..., None, None] + st, h

    h0 = jnp.zeros((bn, G, J, P, N), jnp.float32)
    _, prev = lax.scan(step, h0, (jnp.moveaxis(states, 1, 0), jnp.moveaxis(chunk_decay, 1, 0)))
    prev = jnp.moveaxis(prev, 0, 1)
    y_off = jnp.einsum('bclgn,bcgjpn->bclgjp', Cc, prev) * jnp.exp(a_cum).transpose(0, 1, 4, 2, 3)[..., None]
    return (y_diag + y_off).reshape(bn, S, G, J, P)


def bidirectional_ssd(z, xbc, dt, conv_w, conv_b, a_log, dt_bias, d_skip, ssm_norm):
    bn, S, _ = z.shape
    G, J, P, N = SSM_GROUPS, SSM_HEADS_PER_GROUP, SSM_HEAD_DIM, SSM_STATE
    xbc = jax.nn.silu(depthwise_conv(xbc, conv_w, conv_b)).astype(jnp.float32)
    xs, bm, cm = jnp.split(xbc, [SSM_WIDTH, SSM_WIDTH + G * N], axis=-1)
    xs = xs.reshape(bn, S, G, J, P)
    bm = bm.reshape(bn, S, G, N)
    cm = cm.reshape(bn, S, G, N)
    delta = jax.nn.softplus(dt.astype(jnp.float32).reshape(bn, S, 2, G, J)
                            + dt_bias.astype(jnp.float32).reshape(2, G, J))
    a = -jnp.exp(a_log.astype(jnp.float32)).reshape(2, G, J)
    y_fwd = ssd_chunked(xs, delta[:, :, 0], a[0], bm, cm)
    flip = lambda t: t[:, ::-1]
    y_bwd = flip(ssd_chunked(flip(xs), flip(delta[:, :, 1]), a[1], flip(bm), flip(cm)))
    y = y_fwd + y_bwd + xs * d_skip.astype(jnp.float32).reshape(G, J, 1)
    y = y.reshape(bn, S, SSM_WIDTH) * jax.nn.silu(z.astype(jnp.float32))
    y = y.reshape(bn, S, G, SSM_WIDTH // G)
    y = y * lax.rsqrt(jnp.mean(y * y, axis=-1, keepdims=True) + EPS)
    y = y.reshape(bn, S, SSM_WIDTH) * ssm_norm.astype(jnp.float32)
    return y.astype(z.dtype)


def hybrid_mixer(h, w_in, q_norm, k_norm, conv_w, conv_b, a_log, dt_bias, d_skip, ssm_norm, w_out):
    proj = h @ w_in
    cuts = np.cumsum([ATT_WIDTH, ATT_WIDTH, ATT_WIDTH, SSM_WIDTH, SSM_CONV_CH]).tolist()
    q, k, v, z, xbc, dt = jnp.split(proj, cuts, axis=-1)
    att = dilated_attention(q, k, v, q_norm, k_norm)
    ssm = bidirectional_ssd(z, xbc, dt, conv_w, conv_b, a_log, dt_bias, d_skip, ssm_norm)
    return jnp.concatenate([att, ssm], axis=-1) @ w_out


def conformer_conv(h, pw1_w, pw1_b, dw_w, dw_b, ln_g, ln_b, pw2_w, pw2_b):
    u = h @ pw1_w + pw1_b
    a, g = jnp.split(u, 2, axis=-1)
    u = a * jax.nn.sigmoid(g)
    u = depthwise_conv(u, dw_w, dw_b)
    u = jax.nn.silu(layernorm(u, ln_g, ln_b))
    return u @ pw2_w + pw2_b


def swiglu(h, w_gate, w_up, w_down):
    return (jax.nn.silu(h @ w_gate) * (h @ w_up)) @ w_down


def encoder_trunk(x, p):
    for layer in range(DEPTH):
        if layer % 2 == 0:
            e = layer // 2
            x = x + hybrid_mixer(rmsnorm(x, p['mix_norm'][e]), p['w_in'][e], p['q_norm'][e], p['k_norm'][e],
                                 p['ssm_conv_w'][e], p['ssm_conv_b'][e], p['a_log'][e], p['dt_bias'][e],
                                 p['d_skip'][e], p['ssm_norm'][e], p['w_out'][e])
        else:
            o = layer // 2
            x = x + conformer_conv(rmsnorm(x, p['conf_norm'][o]), p['pw1_w'][o], p['pw1_b'][o], p['dw_w'][o],
                                   p['dw_b'][o], p['ln_g'][o], p['ln_b'][o], p['pw2_w'][o], p['pw2_b'][o])
        x = x + swiglu(rmsnorm(x, p['ffn_norm'][layer]), p['w_gate'][layer], p['w_up'][layer], p['w_down'][layer])
    return x


def setup_inputs(seed: int = 0) -> dict:
    key = jax.random.key(seed)
    ks = jax.random.split(key, 32)
    f32 = jnp.float32
    nrm = lambda k, shape, scale: jax.random.normal(k, shape, f32) * scale
    gain = lambda k, shape: 1.0 + 0.02 * jax.random.normal(k, shape, f32)
    dt0 = jnp.exp(jax.random.uniform(ks[9], (N_EVEN, 2, SSM_HEADS), f32, math.log(1e-3), math.log(1e-1)))
    return {
        'x_prompt': jax.random.normal(ks[0], (BATCH, SEQ, D_MODEL), f32),
        'x_sample': jax.random.normal(ks[1], (DEC_BATCH, DEC_SEQ, D_MODEL), f32),
        'mix_norm': gain(ks[2], (N_EVEN, D_MODEL)),
        'w_in': nrm(ks[3], (N_EVEN, D_MODEL, IN_COLS), D_MODEL ** -0.5),
        'q_norm': gain(ks[4], (N_EVEN, HEAD_DIM)),
        'k_norm': gain(ks[5], (N_EVEN, HEAD_DIM)),
        'ssm_conv_w': nrm(ks[6], (N_EVEN, SSM_CONV, SSM_CONV_CH), SSM_CONV ** -0.5),
        'ssm_conv_b': nrm(ks[7], (N_EVEN, SSM_CONV_CH), 0.01),
        'a_log': jnp.log(jax.random.uniform(ks[8], (N_EVEN, 2, SSM_HEADS), f32, 1.0, 16.0)),
        'dt_bias': dt0 + jnp.log(-jnp.expm1(-dt0)),
        'd_skip': gain(ks[10], (N_EVEN, SSM_HEADS)),
        'ssm_norm': gain(ks[11], (N_EVEN, SSM_WIDTH)),
        'w_out': nrm(ks[12], (N_EVEN, ATT_WIDTH + SSM_WIDTH, D_MODEL), (ATT_WIDTH + SSM_WIDTH) ** -0.5),
        'conf_norm': gain(ks[13], (N_ODD, D_MODEL)),
        'pw1_w': nrm(ks[14], (N_ODD, D_MODEL, 2 * CONF_CH), D_MODEL ** -0.5),
        'pw1_b': nrm(ks[15], (N_ODD, 2 * CONF_CH), 0.01),
        'dw_w': nrm(ks[16], (N_ODD, CONF_KERNEL, CONF_CH), CONF_KERNEL ** -0.5),
        'dw_b': nrm(ks[17], (N_ODD, CONF_CH), 0.01),
        'ln_g': gain(ks[18], (N_ODD, CONF_CH)),
        'ln_b': nrm(ks[19], (N_ODD, CONF_CH), 0.01),
        'pw2_w': nrm(ks[20], (N_ODD, CONF_CH, D_MODEL), CONF_CH ** -0.5),
        'pw2_b': nrm(ks[21], (N_ODD, D_MODEL), 0.01),
        'ffn_norm': gain(ks[22], (DEPTH, D_MODEL)),
        'w_gate': nrm(ks[23], (DEPTH, D_MODEL, FFN_HIDDEN), D_MODEL ** -0.5),
        'w_up': nrm(ks[24], (DEPTH, D_MODEL, FFN_HIDDEN), D_MODEL ** -0.5),
        'w_down': nrm(ks[25], (DEPTH, FFN_HIDDEN, D_MODEL), FFN_HIDDEN ** -0.5),
    }


def reference(x_prompt, x_sample, mix_norm, w_in, q_norm, k_norm, ssm_conv_w, ssm_conv_b, a_log, dt_bias,
              d_skip, ssm_norm, w_out, conf_norm, pw1_w, pw1_b, dw_w, dw_b, ln_g, ln_b, pw2_w, pw2_b,
              ffn_norm, w_gate, w_up, w_down):
    params = dict(mix_norm=mix_norm, w_in=w_in, q_norm=q_norm, k_norm=k_norm, ssm_conv_w=ssm_conv_w,
                  ssm_conv_b=ssm_conv_b, a_log=a_log, dt_bias=dt_bias, d_skip=d_skip, ssm_norm=ssm_norm,
                  w_out=w_out, conf_norm=conf_norm, pw1_w=pw1_w, pw1_b=pw1_b, dw_w=dw_w, dw_b=dw_b,
                  ln_g=ln_g, ln_b=ln_b, pw2_w=pw2_w, pw2_b=pw2_b, ffn_norm=ffn_norm, w_gate=w_gate,
                  w_up=w_up, w_down=w_down)
    y_prompt = encoder_trunk(x_prompt, params)
    y_sample = encoder_trunk(x_sample, params)
    return (y_prompt, y_sample)
```

```python
import functools
import math

import jax
import jax.numpy as jnp
from jax import lax
from jax.experimental import pallas as pl
from jax.experimental.pallas import tpu as pltpu

F32 = jnp.float32
BF16 = jnp.bfloat16

D_MODEL = 2048
DEPTH = 4
HEAD_DIM = 128
ATT_HEADS = 8
ATT_WIDTH = ATT_HEADS * HEAD_DIM
ROPE_DIM = HEAD_DIM // 4
ROPE_THETA = 500000.0
ATT_HALF = 64
SSM_HEAD_DIM = 64
SSM_WIDTH = 1024
SSM_HEADS = 16
SSM_GROUPS = 2
SSM_STATE = 128
SSM_CONV = 4
SSM_CHUNK = 128
SSM_CONV_CH = SSM_WIDTH + 2 * SSM_GROUPS * SSM_STATE
PROJ_COLS = 3 * ATT_WIDTH + SSM_WIDTH + SSM_CONV_CH
XBC_COL0 = 3 * ATT_WIDTH + SSM_WIDTH
CONF_KERNEL = 31
FFN_HIDDEN = 5632
EPS = 1e-6
NEG_INF = -1e30

LANES = 128
VMEM_LIMIT = 48 * 1024 * 1024


def _cparams(sem):
    return pltpu.CompilerParams(dimension_semantics=sem, vmem_limit_bytes=VMEM_LIMIT)


def _rms_rows(x, g):
    ms = jnp.mean(x * x, axis=-1, keepdims=True)
    return x * lax.rsqrt(ms + EPS) * g


def _silu(x):
    return x * (1.0 / (1.0 + jnp.exp(-x)))


def _sigmoid(x):
    return 1.0 / (1.0 + jnp.exp(-x))


def _in_proj_kernel(x_ref, g_ref, w_ref, wdt_ref, hg_ref, ra_ref, rp_ref, rm_ref, o_ref, odt_ref, xn_ref,
                    *, n_qk_tiles, heads_per_tile):
    j = pl.program_id(1)

    @pl.when(j == 0)
    def _():
        xn = _rms_rows(x_ref[...], g_ref[...]).astype(BF16)
        xn_ref[...] = xn
        odt_ref[...] = jnp.dot(xn, wdt_ref[...], preferred_element_type=F32)

    acc = jnp.dot(xn_ref[...], w_ref[...], preferred_element_type=F32)

    @pl.when(j < n_qk_tiles)
    def _():
        ra, rp, rm = ra_ref[...], rp_ref[...], rm_ref[...]
        for hh in range(heads_per_tile):
            sl = slice(hh * HEAD_DIM, (hh + 1) * HEAD_DIM)
            t = _rms_rows(acc[:, sl], hg_ref[:, sl])
            o_ref[:, sl] = (t * ra + pltpu.roll(t, ROPE_DIM // 2, 1) * rp
                            + pltpu.roll(t, HEAD_DIM - ROPE_DIM // 2, 1) * rm)

    @pl.when(j >= n_qk_tiles)
    def _():
        o_ref[...] = acc


def _in_proj(x, g, w, wdt, head_gain, rope, seq, *, tm=1024, tn=512):
    m = x.shape[0]
    n_qk_tiles = 2 * ATT_WIDTH // tn
    pos_tiles = seq // tm
    rope_spec = pl.BlockSpec((tm, HEAD_DIM), lambda i, j: (i % pos_tiles, 0))
    return pl.pallas_call(
        functools.partial(_in_proj_kernel, n_qk_tiles=n_qk_tiles, heads_per_tile=tn // HEAD_DIM),
        grid=(m // tm, PROJ_COLS // tn),
        in_specs=[
            pl.BlockSpec((tm, D_MODEL), lambda i, j: (i, 0)),
            pl.BlockSpec((1, D_MODEL), lambda i, j: (0, 0)),
            pl.BlockSpec((D_MODEL, tn), lambda i, j: (0, j)),
            pl.BlockSpec((D_MODEL, LANES), lambda i, j: (0, 0)),
            pl.BlockSpec((1, tn), lambda i, j: (0, jnp.minimum(j, n_qk_tiles - 1))),
            rope_spec, rope_spec, rope_spec,
        ],
        out_specs=[
            pl.BlockSpec((tm, tn), lambda i, j: (i, j)),
            pl.BlockSpec((tm, LANES), lambda i, j: (i, 0)),
        ],
        out_shape=[jax.ShapeDtypeStruct((m, PROJ_COLS), F32), jax.ShapeDtypeStruct((m, LANES), F32)],
        scratch_shapes=[pltpu.VMEM((tm, D_MODEL), BF16)],
        compiler_params=_cparams(("parallel", "arbitrary")),
        name="in_proj",
    )(x, g, w, wdt, head_gain, *rope)


def _attn_branch(q, k, v, acc, ml, cls_len, seq, first, bq):
    kw = min(bq + 2 * ATT_HALF, cls_len)
    nb = cls_len // bq

    def body(t, carry):
        lane = lax.broadcasted_iota(jnp.int32, (bq, LANES), 1)
        r = t // nb
        b = t - r * nb
        kstart = jnp.clip(b * bq - ATT_HALF, 0, cls_len - kw)
        qoff = pl.multiple_of(r * cls_len + b * bq, ATT_HALF)
        koff = pl.multiple_of(r * cls_len + kstart, ATT_HALF)
        qv = q[pl.ds(qoff, bq), :].astype(BF16)
        kv = k[pl.ds(koff, kw), :].astype(BF16)
        vv = v[pl.ds(koff, kw), :].astype(BF16)
        s = lax.dot_general(qv, kv, (((1,), (1,)), ((), ())), preferred_element_type=F32)
        dist = (lax.broadcasted_iota(jnp.int32, (bq, kw), 0) - lax.broadcasted_iota(jnp.int32, (bq, kw), 1)
                + (b * bq - kstart))
        s = jnp.where(jnp.abs(dist) <= ATT_HALF, s, NEG_INF)
        m_cur = jnp.max(s, axis=-1, keepdims=True)
        if first:
            m_new = m_cur
        else:
            mlv = ml[pl.ds(qoff, bq), :]
            m_old, l_old = mlv[:, 0:1], mlv[:, 64:65]
            m_new = jnp.maximum(m_old, m_cur)
        p = jnp.exp(s - m_new)
        l_new = jnp.sum(p, axis=-1, keepdims=True)
        pv = jnp.dot(p.astype(BF16), vv, preferred_element_type=F32)
        if not first:
            alpha = jnp.exp(m_old - m_new)
            l_new = alpha * l_old + l_new
            pv = alpha * acc[pl.ds(qoff, bq), :] + pv
        acc[pl.ds(qoff, bq), :] = pv
        ml[pl.ds(qoff, bq), :] = jnp.where(lane < 64, m_new, l_new)
        return carry

    lax.fori_loop(0, seq // bq, body, 0)


def _regroup_by_4(srcs, dsts, seq, src_cls):
    sub = src_cls // 4
    ch = min(sub, 128)
    n_per = sub // ch

    def body(t, carry):
        c = t // (4 * n_per)
        rem = t - c * (4 * n_per)
        a = rem // n_per
        i = rem - a * n_per
        n_old = seq // src_cls
        dst_off = pl.multiple_of((a * n_old + c) * sub + i * ch, 8)
        src_off = c * src_cls + a + 4 * i * ch
        for s_ref, d_ref in zip(srcs, dsts):
            d_ref[pl.ds(dst_off, ch), :] = s_ref[pl.ds(src_off, ch, stride=4), :]
        return carry

    lax.fori_loop(0, (seq // src_cls) * 4 * n_per, body, 0)


def _attn_kernel(q_ref, k_ref, v_ref, o_ref, acc0, ml0, q4, k4, v4, acc4, ml4, q16, k16, v16, *, seq, bq):
    _attn_branch(q_ref, k_ref, v_ref, acc0, ml0, seq, seq, True, bq)
    _regroup_by_4((q_ref, k_ref, v_ref, acc0, ml0), (q4, k4, v4, acc4, ml4), seq, seq)
    _attn_branch(q4, k4, v4, acc4, ml4, seq // 4, seq, False, bq)
    _regroup_by_4((q4, k4, v4, acc4, ml4), (q16, k16, v16, acc0, ml0), seq, seq // 4)
    _attn_branch(q16, k16, v16, acc0, ml0, seq // 16, seq, False, min(bq, seq // 16))
    cls_len = seq // 16
    for r in range(16):
        rows = pl.ds(r * cls_len, cls_len)
        out = acc0[rows, :] * (1.0 / ml0[rows, 64:65])
        o_ref[pl.ds(r, cls_len, stride=16), :] = out


def _attention(proj, n_seq, seq, *, bq=128):
    m = proj.shape[0]
    blk = lambda off: pl.BlockSpec((seq, HEAD_DIM), lambda b, h, off=off: (b, off + h))
    return pl.pallas_call(
        functools.partial(_attn_kernel, seq=seq, bq=bq),
        grid=(n_seq, ATT_HEADS),
        in_specs=[blk(0), blk(ATT_HEADS), blk(2 * ATT_HEADS)],
        out_specs=pl.BlockSpec((seq, HEAD_DIM), lambda b, h: (b, h)),
        out_shape=jax.ShapeDtypeStruct((m, ATT_WIDTH), F32),
        scratch_shapes=[pltpu.VMEM((seq, HEAD_DIM), F32) for _ in range(10)],
        compiler_params=_cparams(("parallel", "parallel")),
        name="dilated_attention",
    )(proj, proj, proj)


CONV_PAD = 16
CONV_ROWS = 64


def _dwconv_kernel(x_ref, w_ref, b_ref, o_ref, pad_ref, *, seq, width, act):
    left = (width - 1) // 2
    zeros = jnp.zeros((CONV_PAD, LANES), F32)
    pad_ref[pl.ds(0, CONV_PAD), :] = zeros
    pad_ref[pl.ds(CONV_PAD + seq, CONV_PAD), :] = zeros
    pad_ref[pl.ds(CONV_PAD, seq), :] = x_ref[...]

    def body(i, carry):
        r0 = pl.multiple_of(i * CONV_ROWS, CONV_ROWS)
        acc = jnp.broadcast_to(b_ref[...], (CONV_ROWS, LANES))
        for tap in range(width):
            acc = acc + pad_ref[pl.ds(r0 + CONV_PAD + tap - left, CONV_ROWS), :] * w_ref[tap:tap + 1, :]
        if act:
            acc = _silu(acc)
        o_ref[pl.ds(r0, CONV_ROWS), :] = acc
        return carry

    lax.fori_loop(0, seq // CONV_ROWS, body, 0)


def _dwconv(x, w, b, n_seq, seq, col0, channels, act):
    width = w.shape[0]
    cb0 = col0 // LANES
    return pl.pallas_call(
        functools.partial(_dwconv_kernel, seq=seq, width=width, act=act),
        grid=(n_seq, channels // LANES),
        in_specs=[
            pl.BlockSpec((seq, LANES), lambda s, c: (s, cb0 + c)),
            pl.BlockSpec((width, LANES), lambda s, c: (0, c)),
            pl.BlockSpec((1, LANES), lambda s, c: (0, c)),
        ],
        out_specs=pl.BlockSpec((seq, LANES), lambda s, c: (s, c)),
        out_shape=jax.ShapeDtypeStruct((n_seq * seq, channels), F32),
        scratch_shapes=[pltpu.VMEM((seq + 2 * CONV_PAD, LANES), F32)],
        compiler_params=_cparams(("parallel", "parallel")),
        name=f"dwconv{width}",
    )(x, w, b.reshape(1, channels))


def _split3(x):
    hi = x.astype(BF16)
    r1 = x - hi.astype(F32)
    mid = r1.astype(BF16)
    lo = (r1 - mid.astype(F32)).astype(BF16)
    return hi, mid, lo


def _dot_exact_rhs(a, b_bf16):
    return sum(jnp.dot(p, b_bf16, preferred_element_type=F32) for p in _split3(a))


def _dot_exact_lhs(a_bf16, b):
    return sum(jnp.dot(a_bf16, p, preferred_element_type=F32) for p in _split3(b))


def _softplus(x):
    return jnp.maximum(x, 0.0) + jnp.log(1.0 + jnp.exp(-jnp.abs(x)))


def _pair_expand(col_vals, h0, lane_lt64):
    return jnp.where(lane_lt64, col_vals[:, h0:h0 + 1], col_vals[:, h0 + 1:h0 + 2])


def _ssd_direction(xs_ref, b_ref, c_ref, dt_ref, dtt_ref, brow_ref, bcol_ref, arow_ref, acol_ref,
                   y_ref, h_ref, reverse):
    T = SSM_CHUNK
    d0 = SSM_HEADS if reverse else 0
    delta_r = _softplus(dt_ref[...] + brow_ref[...])
    adt_r = delta_r * (-jnp.exp(arow_ref[...]))
    delta_c = _softplus(dtt_ref[d0:d0 + SSM_HEADS, :] + bcol_ref[d0:d0 + SSM_HEADS, :])
    adt_c = delta_c * (-jnp.exp(acol_ref[d0:d0 + SSM_HEADS, :]))

    ri = lax.broadcasted_iota(jnp.int32, (T, T), 0)
    ci = lax.broadcasted_iota(jnp.int32, (T, T), 1)
    causal = (ci >= ri) if reverse else (ci <= ri)
    tri = jnp.where(causal, 1.0, 0.0).astype(BF16)
    tri_t = jnp.where((ri >= ci) if reverse else (ri <= ci), 1.0, 0.0).astype(BF16)
    cum_r = _dot_exact_lhs(tri, adt_r)
    cum_c = _dot_exact_rhs(adt_c, tri_t)
    end = 0 if reverse else T - 1
    total_r = cum_r[end:end + 1, :]
    dec_end_r = jnp.exp(total_r - cum_r)
    dec_in_r = jnp.exp(cum_r)
    chunk_decay = jnp.exp(total_r)

    lane_lt64 = lax.broadcasted_iota(jnp.int32, (T, LANES), 1) < SSM_HEAD_DIM
    lane_lt64_row = lax.broadcasted_iota(jnp.int32, (1, LANES), 1) < SSM_HEAD_DIM
    pairs_per_group = SSM_HEADS // SSM_GROUPS // 2

    for g in range(SSM_GROUPS):
        bm = b_ref[:, g * SSM_STATE:(g + 1) * SSM_STATE]
        cm = c_ref[:, g * SSM_STATE:(g + 1) * SSM_STATE].astype(BF16)
        bm16 = bm.astype(BF16)
        cb = lax.dot_general(cm, bm16, (((1,), (1,)), ((), ())), preferred_element_type=F32)
        bt = jnp.transpose(bm).astype(BF16)
        h_prev = h_ref[g]
        y_off = jnp.dot(cm, h_prev.astype(BF16), preferred_element_type=F32)
        new_cols = []
        for kk in range(pairs_per_group):
            k = g * pairs_per_group + kk
            lanes = slice(k * LANES, (k + 1) * LANES)
            xs = xs_ref[:, lanes]
            h0 = d0 + 2 * k
            x_dt = xs * _pair_expand(delta_r, h0, lane_lt64)
            x16 = x_dt.astype(BF16)
            ys = []
            for hh in (2 * k, 2 * k + 1):
                seg = cum_r[:, d0 + hh:d0 + hh + 1] - cum_c[hh:hh + 1, :]
                gmat = (cb * jnp.exp(jnp.where(causal, seg, NEG_INF))).astype(BF16)
                ys.append(jnp.dot(gmat, x16, preferred_element_type=F32))
            y_diag = jnp.where(lane_lt64, ys[0], ys[1])
            off = y_off[:, kk * LANES:(kk + 1) * LANES] * _pair_expand(dec_in_r, h0, lane_lt64)
            y_ref[:, lanes] = y_diag + off
            xd = (x_dt * _pair_expand(dec_end_r, h0, lane_lt64)).astype(BF16)
            st = jnp.dot(bt, xd, preferred_element_type=F32)
            cd = _pair_expand(chunk_decay, h0, lane_lt64_row)
            new_cols.append(h_prev[:, kk * LANES:(kk + 1) * LANES] * cd + st)
        for kk in range(pairs_per_group):
            h_ref[g, :, kk * LANES:(kk + 1) * LANES] = new_cols[kk]


def _ssd_kernel(xsf, bf, cf, dtf, dttf, xsb, bb, cb, dtb, dttb, brow, bcol, arow, acol, yf, yb, hf, hb):
    @pl.when(pl.program_id(1) == 0)
    def _():
        hf[...] = jnp.zeros_like(hf)
        hb[...] = jnp.zeros_like(hb)

    _ssd_direction(xsf, bf, cf, dtf, dttf, brow, bcol, arow, acol, yf, hf, False)
    _ssd_direction(xsb, bb, cb, dtb, dttb, brow, bcol, arow, acol, yb, hb, True)


def _ssd(xbc, dt, dtt, dt_bias, a_log, n_seq, seq):
    m = xbc.shape[0]
    T = SSM_CHUNK
    nc = seq // T
    fwd = lambda s, i: s * nc + i
    bwd = lambda s, i: s * nc + (nc - 1 - i)
    gn = SSM_GROUPS * SSM_STATE

    def specs(row):
        return [
            pl.BlockSpec((T, SSM_WIDTH), lambda s, i: (row(s, i), 0)),
            pl.BlockSpec((T, gn), lambda s, i: (row(s, i), SSM_WIDTH // gn)),
            pl.BlockSpec((T, gn), lambda s, i: (row(s, i), SSM_WIDTH // gn + 1)),
            pl.BlockSpec((T, LANES), lambda s, i: (row(s, i), 0)),
            pl.BlockSpec((2 * SSM_HEADS, T), lambda s, i: (0, row(s, i))),
        ]

    small = lambda shape: pl.BlockSpec(shape, lambda s, i: (0, 0))
    nh2 = 2 * SSM_HEADS
    lane_row = lambda v: jnp.pad(v.reshape(1, nh2), ((0, 0), (0, LANES - nh2)))
    return pl.pallas_call(
        _ssd_kernel,
        grid=(n_seq, nc),
        in_specs=specs(fwd) + specs(bwd) + [small((1, LANES)), small((nh2, 1)), small((1, LANES)), small((nh2, 1))],
        out_specs=[
            pl.BlockSpec((T, SSM_WIDTH), lambda s, i: (fwd(s, i), 0)),
            pl.BlockSpec((T, SSM_WIDTH), lambda s, i: (bwd(s, i), 0)),
        ],
        out_shape=[jax.ShapeDtypeStruct((m, SSM_WIDTH), F32)] * 2,
        scratch_shapes=[pltpu.VMEM((SSM_GROUPS, SSM_STATE, SSM_WIDTH // SSM_GROUPS), F32)] * 2,
        compiler_params=_cparams(("parallel", "arbitrary")),
        name="ssd",
    )(xbc, xbc, xbc, dt, dtt, xbc, xbc, xbc, dt, dtt,
      lane_row(dt_bias), dt_bias.reshape(nh2, 1), lane_row(a_log), a_log.reshape(nh2, 1))


def _mix_out_kernel(att_ref, yf_ref, yb_ref, xs_ref, z_ref, dsk_ref, gn_ref, wa_ref, ws_ref, x_ref, o_ref,
                    att16_ref, ssm16_ref):
    @pl.when(pl.program_id(1) == 0)
    def _():
        att16_ref[...] = att_ref[...].astype(BF16)
        y = (yf_ref[...] + yb_ref[...] + xs_ref[...] * dsk_ref[...]) * _silu(z_ref[...])
        gw = SSM_WIDTH // SSM_GROUPS
        for g in range(SSM_GROUPS):
            sl = slice(g * gw, (g + 1) * gw)
            ssm16_ref[:, sl] = _rms_rows(y[:, sl], gn_ref[:, sl]).astype(BF16)

    o_ref[...] = (x_ref[...]
                  + jnp.dot(att16_ref[...], wa_ref[...], preferred_element_type=F32)
                  + jnp.dot(ssm16_ref[...], ws_ref[...], preferred_element_type=F32))


def _mix_out(att, yf, yb, xbc, proj, dskip, gnorm, w_att, w_ssm, x, *, tm=512, tn=512):
    m = x.shape[0]
    row = lambda c: pl.BlockSpec((tm, SSM_WIDTH), lambda i, j, c=c: (i, c))
    vec = pl.BlockSpec((1, SSM_WIDTH), lambda i, j: (0, 0))
    wsp = pl.BlockSpec((SSM_WIDTH, tn), lambda i, j: (0, j))
    return pl.pallas_call(
        _mix_out_kernel,
        grid=(m // tm, D_MODEL // tn),
        in_specs=[row(0), row(0), row(0), row(0), row(3 * ATT_WIDTH // SSM_WIDTH), vec, vec, wsp, wsp,
                  pl.BlockSpec((tm, tn), lambda i, j: (i, j))],
        out_specs=pl.BlockSpec((tm, tn), lambda i, j: (i, j)),
        out_shape=jax.ShapeDtypeStruct((m, D_MODEL), F32),
        scratch_shapes=[pltpu.VMEM((tm, ATT_WIDTH), BF16), pltpu.VMEM((tm, SSM_WIDTH), BF16)],
        compiler_params=_cparams(("parallel", "arbitrary")),
        name="mix_out",
    )(att, yf, yb, xbc, proj, dskip, gnorm, w_att, w_ssm, x)


def _pw1_glu_kernel(x_ref, g_ref, wa_ref, wg_ref, ba_ref, bg_ref, o_ref, xn_ref):
    @pl.when(pl.program_id(1) == 0)
    def _():
        xn_ref[...] = _rms_rows(x_ref[...], g_ref[...]).astype(BF16)

    xn = xn_ref[...]
    a = jnp.dot(xn, wa_ref[...], preferred_element_type=F32) + ba_ref[...]
    gate = jnp.dot(xn, wg_ref[...], preferred_element_type=F32) + bg_ref[...]
    o_ref[...] = a * _sigmoid(gate)


def _pw1_glu(x, g, w, b, *, tm=1024, tn=512):
    m = x.shape[0]
    nt = D_MODEL // tn
    return pl.pallas_call(
        _pw1_glu_kernel,
        grid=(m // tm, nt),
        in_specs=[
            pl.BlockSpec((tm, D_MODEL), lambda i, j: (i, 0)),
            pl.BlockSpec((1, D_MODEL), lambda i, j: (0, 0)),
            pl.BlockSpec((D_MODEL, tn), lambda i, j: (0, j)),
            pl.BlockSpec((D_MODEL, tn), lambda i, j: (0, nt + j)),
            pl.BlockSpec((1, tn), lambda i, j: (0, j)),
            pl.BlockSpec((1, tn), lambda i, j: (0, nt + j)),
        ],
        out_specs=pl.BlockSpec((tm, tn), lambda i, j: (i, j)),
        out_shape=jax.ShapeDtypeStruct((m, D_MODEL), F32),
        scratch_shapes=[pltpu.VMEM((tm, D_MODEL), BF16)],
        compiler_params=_cparams(("parallel", "arbitrary")),
        name="pw1_glu",
    )(x, g, w, w, b, b)


def _pw2_kernel(u_ref, lg_ref, lb_ref, w_ref, b_ref, x_ref, o_ref, un_ref):
    @pl.when(pl.program_id(1) == 0)
    def _():
        u = u_ref[...]
        mu = jnp.mean(u, axis=-1, keepdims=True)
        uc = u - mu
        var = jnp.mean(uc * uc, axis=-1, keepdims=True)
        un_ref[...] = _silu(uc * lax.rsqrt(var + EPS) * lg_ref[...] + lb_ref[...]).astype(BF16)

    o_ref[...] = x_ref[...] + b_ref[...] + jnp.dot(un_ref[...], w_ref[...], preferred_element_type=F32)


def _pw2(u, ln_g, ln_b, w, b, x, *, tm=1024, tn=512):
    m = x.shape[0]
    vec = pl.BlockSpec((1, D_MODEL), lambda i, j: (0, 0))
    return pl.pallas_call(
        _pw2_kernel,
        grid=(m // tm, D_MODEL // tn),
        in_specs=[
            pl.BlockSpec((tm, D_MODEL), lambda i, j: (i, 0)), vec, vec,
            pl.BlockSpec((D_MODEL, tn), lambda i, j: (0, j)),
            pl.BlockSpec((1, tn), lambda i, j: (0, j)),
            pl.BlockSpec((tm, tn), lambda i, j: (i, j)),
        ],
        out_specs=pl.BlockSpec((tm, tn), lambda i, j: (i, j)),
        out_shape=jax.ShapeDtypeStruct((m, D_MODEL), F32),
        scratch_shapes=[pltpu.VMEM((tm, D_MODEL), BF16)],
        compiler_params=_cparams(("parallel", "arbitrary")),
        name="pw2",
    )(u, ln_g, ln_b, w, b, x)


def _ffn_kernel(x_ref, g_ref, wg_ref, wu_ref, wd_ref, o_ref, xn_ref, acc_ref):
    j = pl.program_id(1)

    @pl.when(j == 0)
    def _():
        xn_ref[...] = _rms_rows(x_ref[...], g_ref[...]).astype(BF16)

    xn = xn_ref[...]
    gate = jnp.dot(xn, wg_ref[...], preferred_element_type=F32)
    up = jnp.dot(xn, wu_ref[...], preferred_element_type=F32)
    h = (_silu(gate) * up).astype(BF16)
    part = jnp.dot(h, wd_ref[...], preferred_element_type=F32)

    @pl.when(j == 0)
    def _():
        acc_ref[...] = part

    @pl.when(j > 0)
    def _():
        acc_ref[...] += part

    @pl.when(j == pl.num_programs(1) - 1)
    def _():
        o_ref[...] = x_ref[...] + acc_ref[...]


def _ffn(x, g, wg, wu, wd, *, tm=512, th=512):
    m = x.shape[0]
    return pl.pallas_call(
        _ffn_kernel,
        grid=(m // tm, FFN_HIDDEN // th),
        in_specs=[
            pl.BlockSpec((tm, D_MODEL), lambda i, j: (i, 0)),
            pl.BlockSpec((1, D_MODEL), lambda i, j: (0, 0)),
            pl.BlockSpec((D_MODEL, th), lambda i, j: (0, j)),
            pl.BlockSpec((D_MODEL, th), lambda i, j: (0, j)),
            pl.BlockSpec((th, D_MODEL), lambda i, j: (j, 0)),
        ],
        out_specs=pl.BlockSpec((tm, D_MODEL), lambda i, j: (i, 0)),
        out_shape=jax.ShapeDtypeStruct((m, D_MODEL), F32),
        scratch_shapes=[pltpu.VMEM((tm, D_MODEL), BF16), pltpu.VMEM((tm, D_MODEL), F32)],
        compiler_params=_cparams(("parallel", "arbitrary")),
        name="ffn",
    )(x, g, wg, wu, wd)


def _rope_tables(seq):
    pos = jnp.arange(seq, dtype=F32)
    inv_freq = ROPE_THETA ** (-jnp.arange(0, ROPE_DIM, 2, dtype=F32) / ROPE_DIM)
    ang = pos[:, None] * inv_freq[None, :]
    cos, sin = jnp.cos(ang), jnp.sin(ang)
    half = ROPE_DIM // 2
    zeros = lambda n: jnp.zeros((seq, n), F32)
    keep = jnp.concatenate([cos, cos, jnp.ones((seq, HEAD_DIM - ROPE_DIM), F32)], axis=-1)
    from_prev = jnp.concatenate([zeros(half), sin, zeros(HEAD_DIM - ROPE_DIM)], axis=-1)
    from_next = jnp.concatenate([-sin, zeros(HEAD_DIM - half)], axis=-1)
    return keep, from_prev, from_next


def _trunk(x3, p):
    n_seq, seq, _ = x3.shape
    x = x3.reshape(n_seq * seq, D_MODEL)
    rope = _rope_tables(seq)
    row = lambda v: v.reshape(1, -1)
    for layer in range(DEPTH):
        if layer % 2 == 0:
            e = layer // 2
            w_in = p['w_in'][e]
            w_main = w_in[:, :PROJ_COLS].astype(BF16)
            w_dt = jnp.pad(w_in[:, PROJ_COLS:], ((0, 0), (0, LANES - 2 * SSM_HEADS))).astype(BF16)
            head_gain = jnp.concatenate([jnp.tile(p['q_norm'][e] * (HEAD_DIM ** -0.5), ATT_HEADS),
                                         jnp.tile(p['k_norm'][e], ATT_HEADS)]).reshape(1, 2 * ATT_WIDTH)
            proj, dt = _in_proj(x, row(p['mix_norm'][e]), w_main, w_dt, head_gain, rope, seq)
            att = _attention(proj, n_seq, seq)
            xbc = _dwconv(proj, p['ssm_conv_w'][e], p['ssm_conv_b'][e], n_seq, seq, XBC_COL0, SSM_CONV_CH, True)
            dtt = jnp.transpose(dt[:, :2 * SSM_HEADS])
            yf, yb = _ssd(xbc, dt, dtt, p['dt_bias'][e], p['a_log'][e], n_seq, seq)
            w_out = p['w_out'][e].astype(BF16)
            dskip = jnp.repeat(p['d_skip'][e], SSM_HEAD_DIM).reshape(1, SSM_WIDTH)
            x = _mix_out(att, yf, yb, xbc, proj, dskip, row(p['ssm_norm'][e]),
                         w_out[:ATT_WIDTH], w_out[ATT_WIDTH:], x)
        else:
            o = layer // 2
            u = _pw1_glu(x, row(p['conf_norm'][o]), p['pw1_w'][o].astype(BF16), row(p['pw1_b'][o]))
            u = _dwconv(u, p['dw_w'][o], p['dw_b'][o], n_seq, seq, 0, D_MODEL, False)
            x = _pw2(u, row(p['ln_g'][o]), row(p['ln_b'][o]), p['pw2_w'][o].astype(BF16), row(p['pw2_b'][o]), x)
        x = _ffn(x, row(p['ffn_norm'][layer]), p['w_gate'][layer].astype(BF16), p['w_up'][layer].astype(BF16),
                 p['w_down'][layer].astype(BF16))
    return x.reshape(n_seq, seq, D_MODEL)


def kernel(x_prompt, x_sample, mix_norm, w_in, q_norm, k_norm, ssm_conv_w, ssm_conv_b, a_log, dt_bias, d_skip, ssm_norm, w_out, conf_norm, pw1_w, pw1_b, dw_w, dw_b, ln_g, ln_b, pw2_w, pw2_b, ffn_norm, w_gate, w_up, w_down):
    params = dict(mix_norm=mix_norm, w_in=w_in, q_norm=q_norm, k_norm=k_norm, ssm_conv_w=ssm_conv_w,
                  ssm_conv_b=ssm_conv_b, a_log=a_log, dt_bias=dt_bias, d_skip=d_skip, ssm_norm=ssm_norm,
                  w_out=w_out, conf_norm=conf_norm, pw1_w=pw1_w, pw1_b=pw1_b, dw_w=dw_w, dw_b=dw_b,
                  ln_g=ln_g, ln_b=ln_b, pw2_w=pw2_w, pw2_b=pw2_b, ffn_norm=ffn_norm, w_gate=w_gate,
                  w_up=w_up, w_down=w_down)
    return (_trunk(x_prompt, params), _trunk(x_sample, params))
```

```python
import functools
import math

import jax
import jax.numpy as jnp
from jax import lax
from jax.experimental import pallas as pl
from jax.experimental.pallas import tpu as pltpu

F32 = jnp.float32
BF16 = jnp.bfloat16

D_MODEL = 2048
DEPTH = 4
HEAD_DIM = 128
ATT_HEADS = 8
ATT_WIDTH = ATT_HEADS * HEAD_DIM
ROPE_DIM = HEAD_DIM // 4
ROPE_THETA = 500000.0
ATT_HALF = 64
SSM_HEAD_DIM = 64
SSM_WIDTH = 1024
SSM_HEADS = 16
SSM_GROUPS = 2
SSM_STATE = 128
SSM_CONV = 4
SSM_CHUNK = 128
SSM_CONV_CH = SSM_WIDTH + 2 * SSM_GROUPS * SSM_STATE
PROJ_COLS = 3 * ATT_WIDTH + SSM_WIDTH + SSM_CONV_CH
XBC_COL0 = 3 * ATT_WIDTH + SSM_WIDTH
CONF_KERNEL = 31
FFN_HIDDEN = 5632
EPS = 1e-6
NEG_INF = -1e30

LANES = 128
VMEM_LIMIT = 48 * 1024 * 1024


def _cparams(sem):
    return pltpu.CompilerParams(dimension_semantics=sem, vmem_limit_bytes=VMEM_LIMIT)


def _rms_rows(x, g):
    ms = jnp.mean(x * x, axis=-1, keepdims=True)
    return x * lax.rsqrt(ms + EPS) * g


def _silu(x):
    return x * (1.0 / (1.0 + jnp.exp(-x)))


def _sigmoid(x):
    return 1.0 / (1.0 + jnp.exp(-x))


QK_COLS = 2 * ATT_WIDTH
ZX_COLS = SSM_WIDTH + SSM_CONV_CH


def _in_proj_kernel(x_ref, g_ref, wqk_ref, wv_ref, wzx_ref, wdt_ref, hg_ref, ra_ref, rp_ref, rm_ref,
                    oqk_ref, ov_ref, ozx_ref, odt_ref, xn_ref):
    @pl.when(pl.program_id(1) == 0)
    def _():
        xn = _rms_rows(x_ref[...], g_ref[...]).astype(BF16)
        xn_ref[...] = xn
        odt_ref[...] = jnp.dot(xn, wdt_ref[...], preferred_element_type=F32)

    xn = xn_ref[...]
    acc = jnp.dot(xn, wqk_ref[...], preferred_element_type=F32)
    ov_ref[...] = jnp.dot(xn, wv_ref[...], preferred_element_type=F32)
    ozx_ref[...] = jnp.dot(xn, wzx_ref[...], preferred_element_type=F32).astype(BF16)
    ra, rp, rm = ra_ref[...], rp_ref[...], rm_ref[...]
    for hh in range(acc.shape[1] // HEAD_DIM):
        sl = slice(hh * HEAD_DIM, (hh + 1) * HEAD_DIM)
        t = _rms_rows(acc[:, sl], hg_ref[:, sl])
        oqk_ref[:, sl] = (t * ra + pltpu.roll(t, ROPE_DIM // 2, 1) * rp
                          + pltpu.roll(t, HEAD_DIM - ROPE_DIM // 2, 1) * rm)


def _in_proj(x, g, w_qk, w_v, w_zx, wdt, li, head_gain, rope, seq, *, tm=512, steps=4):
    m = x.shape[0]
    tq, tv, tz = QK_COLS // steps, ATT_WIDTH // steps, ZX_COLS // steps
    pos_tiles = seq // tm
    rope_spec = pl.BlockSpec((tm, HEAD_DIM), lambda i, j: (i % pos_tiles, 0))
    wspec = lambda t: pl.BlockSpec((None, D_MODEL, t), lambda i, j: (li, 0, j))
    ospec = lambda t: pl.BlockSpec((tm, t), lambda i, j: (i, j))
    return pl.pallas_call(
        _in_proj_kernel,
        grid=(m // tm, steps),
        in_specs=[
            pl.BlockSpec((tm, D_MODEL), lambda i, j: (i, 0)),
            pl.BlockSpec((1, D_MODEL), lambda i, j: (0, 0)),
            wspec(tq), wspec(tv), wspec(tz),
            pl.BlockSpec((None, D_MODEL, LANES), lambda i, j: (li, 0, 0)),
            pl.BlockSpec((1, tq), lambda i, j: (0, j)),
            rope_spec, rope_spec, rope_spec,
        ],
        out_specs=[ospec(tq), ospec(tv), ospec(tz), pl.BlockSpec((tm, LANES), lambda i, j: (i, 0))],
        out_shape=[jax.ShapeDtypeStruct((m, QK_COLS), F32), jax.ShapeDtypeStruct((m, ATT_WIDTH), F32),
                   jax.ShapeDtypeStruct((m, ZX_COLS), BF16), jax.ShapeDtypeStruct((m, LANES), F32)],
        scratch_shapes=[pltpu.VMEM((tm, D_MODEL), BF16)],
        compiler_params=_cparams(("parallel", "arbitrary")),
        name="in_proj",
    )(x, g, w_qk, w_v, w_zx, wdt, head_gain, *rope)


ATT_BQ = 128
ATT_KW = ATT_BQ + 2 * ATT_HALF
ATT_UNROLL = 8


def _attn_branch(q, k, v, acc, den, mx, bias, cls_len, seq, first):
    bq = min(ATT_BQ, cls_len)
    kw = min(ATT_KW, cls_len)
    nb = cls_len // bq
    ones = jnp.ones((kw, HEAD_DIM), BF16)

    def load(t):
        r = t // nb
        b = t - r * nb
        kstart = jnp.clip(b * bq - ATT_HALF, 0, cls_len - kw)
        qoff = pl.multiple_of(r * cls_len + b * bq, ATT_HALF)
        koff = pl.multiple_of(r * cls_len + kstart, ATT_HALF)
        case = (b * bq - kstart) // ATT_HALF
        qv = q[pl.ds(qoff, bq), :].astype(BF16)
        kv = k[pl.ds(koff, kw), :].astype(BF16)
        vv = v[pl.ds(koff, kw), :].astype(BF16)
        bias_t = bias[case, :bq, :kw]
        old = None if first else (mx[pl.ds(qoff, bq), :], den[pl.ds(qoff, bq), :], acc[pl.ds(qoff, bq), :])
        return qoff, qv, kv, vv, bias_t, old

    def compute(qv, kv, vv, bias_t, old):
        s = lax.dot_general(qv, kv, (((1,), (1,)), ((), ())), preferred_element_type=F32) + bias_t
        m_new = jnp.broadcast_to(jnp.max(s, axis=-1, keepdims=True), (bq, HEAD_DIM))
        if old is not None:
            m_new = jnp.maximum(old[0], m_new)
        p = jnp.exp(s - jnp.concatenate([m_new] * (kw // HEAD_DIM), axis=1))
        pv = jnp.dot(p.astype(BF16), jnp.concatenate([vv, ones], axis=1), preferred_element_type=F32)
        num, rowsum = pv[:, :HEAD_DIM], pv[:, HEAD_DIM:]
        if old is not None:
            alpha = jnp.exp(old[0] - m_new)
            num = alpha * old[2] + num
            rowsum = alpha * old[1] + rowsum
        return m_new, rowsum, num

    def body(t, carry):
        loaded = [load(t * ATT_UNROLL + u) for u in range(ATT_UNROLL)]
        results = [compute(*item[1:]) for item in loaded]
        for (qoff, *_), (m_new, d_new, a_new) in zip(loaded, results):
            mx[pl.ds(qoff, bq), :] = m_new
            den[pl.ds(qoff, bq), :] = d_new
            acc[pl.ds(qoff, bq), :] = a_new
        return carry

    lax.fori_loop(0, seq // bq // ATT_UNROLL, body, 0)


def _regroup_by_4(srcs, dsts, seq, src_cls):
    sub = src_cls // 4
    ch = min(sub, 128)
    n_per = sub // ch
    n_old = seq // src_cls

    def body(t, carry):
        c = t // (4 * n_per)
        rem = t - c * (4 * n_per)
        a = rem // n_per
        i = rem - a * n_per
        dst_off = pl.multiple_of((a * n_old + c) * sub + i * ch, 8)
        src_off = c * src_cls + a + 4 * i * ch
        for s_ref, d_ref in zip(srcs, dsts):
            d_ref[pl.ds(dst_off, ch), :] = s_ref[pl.ds(src_off, ch, stride=4), :]
        return carry

    lax.fori_loop(0, n_old * 4 * n_per, body, 0, unroll=4)


def _attn_kernel(q_ref, k_ref, v_ref, o_ref, acc0, den0, mx0, q4, k4, v4, acc4, den4, mx4, q16, k16, v16, bias,
                 *, seq):
    dist = (lax.broadcasted_iota(jnp.int32, (ATT_BQ, ATT_KW), 0)
            - lax.broadcasted_iota(jnp.int32, (ATT_BQ, ATT_KW), 1))
    for case in range(3):
        bias[case] = jnp.where(jnp.abs(dist + case * ATT_HALF) <= ATT_HALF, 0.0, NEG_INF)
    _attn_branch(q_ref, k_ref, v_ref, acc0, den0, mx0, bias, seq, seq, True)
    _regroup_by_4((q_ref, k_ref, v_ref, acc0, den0, mx0), (q4, k4, v4, acc4, den4, mx4), seq, seq)
    _attn_branch(q4, k4, v4, acc4, den4, mx4, bias, seq // 4, seq, False)
    _regroup_by_4((q4, k4, v4, acc4, den4, mx4), (q16, k16, v16, acc0, den0, mx0), seq, seq // 4)
    _attn_branch(q16, k16, v16, acc0, den0, mx0, bias, seq // 16, seq, False)
    cls_len = seq // 16
    for r in range(16):
        rows = pl.ds(r * cls_len, cls_len)
        q4[pl.ds(r, cls_len, stride=16), :] = acc0[rows, :] * (1.0 / den0[rows, :])
    o_ref[...] = q4[...].astype(o_ref.dtype)


def _attention(qk, v, n_seq, seq):
    m = qk.shape[0]
    blk = lambda off: pl.BlockSpec((seq, HEAD_DIM), lambda b, h, off=off: (b, off + h))
    return pl.pallas_call(
        functools.partial(_attn_kernel, seq=seq),
        grid=(n_seq, ATT_HEADS),
        in_specs=[blk(0), blk(ATT_HEADS), blk(0)],
        out_specs=pl.BlockSpec((seq, HEAD_DIM), lambda b, h: (b, h)),
        out_shape=jax.ShapeDtypeStruct((m, ATT_WIDTH), BF16),
        scratch_shapes=([pltpu.VMEM((seq, HEAD_DIM), F32) for _ in range(12)]
                        + [pltpu.VMEM((3, ATT_BQ, ATT_KW), F32)]),
        compiler_params=_cparams(("parallel", "parallel")),
        name="dilated_attention",
    )(qk, qk, v)


CONV_PAD = 16
CONV_ROWS = 64


def _dwconv_kernel(x_ref, w_ref, b_ref, o_ref, pad_ref, *, seq, width, act):
    left = (width - 1) // 2
    zeros = jnp.zeros((CONV_PAD, LANES), F32)
    pad_ref[pl.ds(0, CONV_PAD), :] = zeros
    pad_ref[pl.ds(CONV_PAD + seq, CONV_PAD), :] = zeros
    pad_ref[pl.ds(CONV_PAD, seq), :] = x_ref[...].astype(F32)

    def body(i, carry):
        r0 = pl.multiple_of(i * CONV_ROWS, CONV_ROWS)
        acc = jnp.broadcast_to(b_ref[...], (CONV_ROWS, LANES))
        for tap in range(width):
            acc = acc + pad_ref[pl.ds(r0 + CONV_PAD + tap - left, CONV_ROWS), :] * w_ref[tap:tap + 1, :]
        if act:
            acc = _silu(acc)
        o_ref[pl.ds(r0, CONV_ROWS), :] = acc.astype(o_ref.dtype)
        return carry

    lax.fori_loop(0, seq // CONV_ROWS, body, 0)


def _dwconv(x, w, b, n_seq, seq, col0, channels, act, out_dtype):
    width = w.shape[0]
    cb0 = col0 // LANES
    return pl.pallas_call(
        functools.partial(_dwconv_kernel, seq=seq, width=width, act=act),
        grid=(n_seq, channels // LANES),
        in_specs=[
            pl.BlockSpec((seq, LANES), lambda s, c: (s, cb0 + c)),
            pl.BlockSpec((width, LANES), lambda s, c: (0, c)),
            pl.BlockSpec((1, LANES), lambda s, c: (0, c)),
        ],
        out_specs=pl.BlockSpec((seq, LANES), lambda s, c: (s, c)),
        out_shape=jax.ShapeDtypeStruct((n_seq * seq, channels), out_dtype),
        scratch_shapes=[pltpu.VMEM((seq + 2 * CONV_PAD, LANES), F32)],
        compiler_params=_cparams(("parallel", "parallel")),
        name=f"dwconv{width}",
    )(x, w, b.reshape(1, channels))


def _split3(x):
    hi = x.astype(BF16)
    r1 = x - hi.astype(F32)
    mid = r1.astype(BF16)
    lo = (r1 - mid.astype(F32)).astype(BF16)
    return hi, mid, lo


def _dot_exact_rhs(a, b_bf16):
    return sum(jnp.dot(p, b_bf16, preferred_element_type=F32) for p in _split3(a))


def _dot_exact_lhs(a_bf16, b):
    return sum(jnp.dot(a_bf16, p, preferred_element_type=F32) for p in _split3(b))


def _softplus(x):
    return jnp.maximum(x, 0.0) + jnp.log(1.0 + jnp.exp(-jnp.abs(x)))


def _ssd_direction(xs_ref, b_ref, c_ref, dt_ref, dtt_ref, brow_ref, bcol_ref, arow_ref, acol_ref,
                   y_ref, h_ref, reverse):
    T = SSM_CHUNK
    d0 = SSM_HEADS if reverse else 0
    adt_r = _softplus(dt_ref[...] + brow_ref[...]) * (-jnp.exp(arow_ref[...]))
    delta_c = _softplus(dtt_ref[d0:d0 + SSM_HEADS, :] + bcol_ref[d0:d0 + SSM_HEADS, :])
    adt_c = delta_c * (-jnp.exp(acol_ref[d0:d0 + SSM_HEADS, :]))

    ri = lax.broadcasted_iota(jnp.int32, (T, T), 0)
    ci = lax.broadcasted_iota(jnp.int32, (T, T), 1)
    causal = (ci >= ri) if reverse else (ci <= ri)
    tri = jnp.where(causal, 1.0, 0.0).astype(BF16)
    tri_t = jnp.where((ri >= ci) if reverse else (ri <= ci), 1.0, 0.0).astype(BF16)
    cum_r = _dot_exact_lhs(tri, adt_r)
    cum_c = _dot_exact_rhs(adt_c, tri_t)
    end = 0 if reverse else T - 1
    total_c = cum_c[:, end:end + 1]
    src_w_c = delta_c * jnp.exp(total_c - cum_c)
    chunk_decay_c = jnp.exp(total_c)

    lane_lt64 = lax.broadcasted_iota(jnp.int32, (T, LANES), 1) < SSM_HEAD_DIM
    pairs_per_group = SSM_HEADS // SSM_GROUPS // 2

    for g in range(SSM_GROUPS):
        bm = b_ref[:, g * SSM_STATE:(g + 1) * SSM_STATE]
        cm = c_ref[:, g * SSM_STATE:(g + 1) * SSM_STATE]
        cb = lax.dot_general(cm, bm, (((1,), (1,)), ((), ())), preferred_element_type=F32)
        bt = jnp.transpose(bm.astype(F32))
        cm32 = cm.astype(F32)
        for kk in range(pairs_per_group):
            k = g * pairs_per_group + kk
            lanes = slice(k * LANES, (k + 1) * LANES)
            cols = slice(kk * LANES, (kk + 1) * LANES)
            xs = xs_ref[:, lanes]
            h_in = h_ref[g, :, cols]
            rhs = jnp.concatenate([xs, h_in.astype(BF16)], axis=0)
            ys, sts = [], []
            for hh in (2 * k, 2 * k + 1):
                a_l = jnp.broadcast_to(cum_r[:, d0 + hh:d0 + hh + 1], (T, T))
                decay = jnp.exp(jnp.where(causal, a_l - cum_c[hh:hh + 1, :], NEG_INF))
                gmat = cb * decay * delta_c[hh:hh + 1, :]
                lhs = jnp.concatenate([gmat, cm32 * jnp.exp(a_l)], axis=1).astype(BF16)
                ys.append(jnp.dot(lhs, rhs, preferred_element_type=F32))
                bts = (bt * src_w_c[hh:hh + 1, :]).astype(BF16)
                sts.append(jnp.dot(bts, xs, preferred_element_type=F32))
            y_ref[:, lanes] = jnp.where(lane_lt64, ys[0], ys[1]).astype(y_ref.dtype)
            cd = jnp.where(lane_lt64, chunk_decay_c[2 * k:2 * k + 1, :], chunk_decay_c[2 * k + 1:2 * k + 2, :])
            h_ref[g, :, cols] = h_in * cd + jnp.where(lane_lt64, sts[0], sts[1])


def _ssd_kernel(xsf, bf, cf, dtf, dttf, xsb, bb, cb, dtb, dttb, brow, bcol, arow, acol, yf, yb, hf, hb):
    @pl.when(pl.program_id(1) == 0)
    def _():
        hf[...] = jnp.zeros_like(hf)
        hb[...] = jnp.zeros_like(hb)

    _ssd_direction(xsf, bf, cf, dtf, dttf, brow, bcol, arow, acol, yf, hf, False)
    _ssd_direction(xsb, bb, cb, dtb, dttb, brow, bcol, arow, acol, yb, hb, True)


def _ssd(xbc, dt, dtt, dt_bias, a_log, n_seq, seq):
    m = xbc.shape[0]
    T = SSM_CHUNK
    nc = seq // T
    fwd = lambda s, i: s * nc + i
    bwd = lambda s, i: s * nc + (nc - 1 - i)
    gn = SSM_GROUPS * SSM_STATE

    def specs(row):
        return [
            pl.BlockSpec((T, SSM_WIDTH), lambda s, i: (row(s, i), 0)),
            pl.BlockSpec((T, gn), lambda s, i: (row(s, i), SSM_WIDTH // gn)),
            pl.BlockSpec((T, gn), lambda s, i: (row(s, i), SSM_WIDTH // gn + 1)),
            pl.BlockSpec((T, LANES), lambda s, i: (row(s, i), 0)),
            pl.BlockSpec((2 * SSM_HEADS, T), lambda s, i: (0, row(s, i))),
        ]

    small = lambda shape: pl.BlockSpec(shape, lambda s, i: (0, 0))
    nh2 = 2 * SSM_HEADS
    lane_row = lambda v: jnp.pad(v.reshape(1, nh2), ((0, 0), (0, LANES - nh2)))
    return pl.pallas_call(
        _ssd_kernel,
        grid=(n_seq, nc),
        in_specs=specs(fwd) + specs(bwd) + [small((1, LANES)), small((nh2, 1)), small((1, LANES)), small((nh2, 1))],
        out_specs=[
            pl.BlockSpec((T, SSM_WIDTH), lambda s, i: (fwd(s, i), 0)),
            pl.BlockSpec((T, SSM_WIDTH), lambda s, i: (bwd(s, i), 0)),
        ],
        out_shape=[jax.ShapeDtypeStruct((m, SSM_WIDTH), BF16)] * 2,
        scratch_shapes=[pltpu.VMEM((SSM_GROUPS, SSM_STATE, SSM_WIDTH // SSM_GROUPS), F32)] * 2,
        compiler_params=_cparams(("parallel", "arbitrary")),
        name="ssd",
    )(xbc, xbc, xbc, dt, dtt, xbc, xbc, xbc, dt, dtt,
      lane_row(dt_bias), dt_bias.reshape(nh2, 1), lane_row(a_log), a_log.reshape(nh2, 1))


ROW_CHUNK = 128


def _mix_out_kernel(att_ref, yf_ref, yb_ref, xs_ref, z_ref, dsk_ref, gn_ref, wa_ref, ws_ref, x_ref, o_ref):
    gw = SSM_WIDTH // SSM_GROUPS
    for c in range(x_ref.shape[0] // ROW_CHUNK):
        rows = slice(c * ROW_CHUNK, (c + 1) * ROW_CHUNK)
        y = ((yf_ref[rows, :].astype(F32) + yb_ref[rows, :].astype(F32) + xs_ref[rows, :].astype(F32) * dsk_ref[...])
             * _silu(z_ref[rows, :].astype(F32)))
        ssm = jnp.concatenate([_rms_rows(y[:, g * gw:(g + 1) * gw], gn_ref[:, g * gw:(g + 1) * gw])
                               for g in range(SSM_GROUPS)], axis=1).astype(BF16)
        o_ref[rows, :] = (x_ref[rows, :]
                          + jnp.dot(att_ref[rows, :], wa_ref[...], preferred_element_type=F32)
                          + jnp.dot(ssm, ws_ref[...], preferred_element_type=F32))


def _mix_out(att, yf, yb, xbc, zx, dskip, gnorm, w_att, w_ssm, li, x, *, tm=512):
    m = x.shape[0]
    row = pl.BlockSpec((tm, SSM_WIDTH), lambda i: (i, 0))
    vec = pl.BlockSpec((1, SSM_WIDTH), lambda i: (0, 0))
    wsp = pl.BlockSpec((None, SSM_WIDTH, D_MODEL), lambda i: (li, 0, 0))
    full = pl.BlockSpec((tm, D_MODEL), lambda i: (i, 0))
    return pl.pallas_call(
        _mix_out_kernel,
        grid=(m // tm,),
        in_specs=[row, row, row, row, row, vec, vec, wsp, wsp, full],
        out_specs=full,
        out_shape=jax.ShapeDtypeStruct((m, D_MODEL), F32),
        compiler_params=_cparams(("parallel",)),
        name="mix_out",
    )(att, yf, yb, xbc, zx, dskip, gnorm, w_att, w_ssm, x)


def _pw1_glu_kernel(x_ref, g_ref, wa_ref, wg_ref, ba_ref, bg_ref, o_ref, xn_ref):
    @pl.when(pl.program_id(1) == 0)
    def _():
        xn_ref[...] = _rms_rows(x_ref[...], g_ref[...]).astype(BF16)

    xn = xn_ref[...]
    a = jnp.dot(xn, wa_ref[...], preferred_element_type=F32) + ba_ref[...]
    gate = jnp.dot(xn, wg_ref[...], preferred_element_type=F32) + bg_ref[...]
    o_ref[...] = a * _sigmoid(gate)


def _pw1_glu(x, g, w, li, b, *, tm=1024, tn=512):
    m = x.shape[0]
    nt = D_MODEL // tn
    return pl.pallas_call(
        _pw1_glu_kernel,
        grid=(m // tm, nt),
        in_specs=[
            pl.BlockSpec((tm, D_MODEL), lambda i, j: (i, 0)),
            pl.BlockSpec((1, D_MODEL), lambda i, j: (0, 0)),
            pl.BlockSpec((None, D_MODEL, tn), lambda i, j: (li, 0, j)),
            pl.BlockSpec((None, D_MODEL, tn), lambda i, j: (li, 0, nt + j)),
            pl.BlockSpec((1, tn), lambda i, j: (0, j)),
            pl.BlockSpec((1, tn), lambda i, j: (0, nt + j)),
        ],
        out_specs=pl.BlockSpec((tm, tn), lambda i, j: (i, j)),
        out_shape=jax.ShapeDtypeStruct((m, D_MODEL), F32),
        scratch_shapes=[pltpu.VMEM((tm, D_MODEL), BF16)],
        compiler_params=_cparams(("parallel", "arbitrary")),
        name="pw1_glu",
    )(x, g, w, w, b, b)


def _pw2_kernel(u_ref, lg_ref, lb_ref, w_ref, b_ref, x_ref, o_ref):
    for c in range(x_ref.shape[0] // ROW_CHUNK):
        rows = slice(c * ROW_CHUNK, (c + 1) * ROW_CHUNK)
        u = u_ref[rows, :]
        mu = jnp.mean(u, axis=-1, keepdims=True)
        uc = u - mu
        var = jnp.mean(uc * uc, axis=-1, keepdims=True)
        un = _silu(uc * lax.rsqrt(var + EPS) * lg_ref[...] + lb_ref[...]).astype(BF16)
        o_ref[rows, :] = x_ref[rows, :] + b_ref[...] + jnp.dot(un, w_ref[...], preferred_element_type=F32)


def _pw2(u, ln_g, ln_b, w, li, b, x, *, tm=512):
    m = x.shape[0]
    vec = pl.BlockSpec((1, D_MODEL), lambda i: (0, 0))
    full = pl.BlockSpec((tm, D_MODEL), lambda i: (i, 0))
    return pl.pallas_call(
        _pw2_kernel,
        grid=(m // tm,),
        in_specs=[full, vec, vec, pl.BlockSpec((None, D_MODEL, D_MODEL), lambda i: (li, 0, 0)), vec, full],
        out_specs=full,
        out_shape=jax.ShapeDtypeStruct((m, D_MODEL), F32),
        compiler_params=_cparams(("parallel",)),
        name="pw2",
    )(u, ln_g, ln_b, w, b, x)


def _ffn_kernel(x_ref, g_ref, wg_ref, wu_ref, wd_ref, o_ref, xn_ref):
    @pl.when(pl.program_id(1) == 0)
    def _():
        x = x_ref[...]
        xn_ref[...] = _rms_rows(x, g_ref[...]).astype(BF16)
        o_ref[...] = x

    xn = xn_ref[...]
    gate = jnp.dot(xn, wg_ref[...], preferred_element_type=F32)
    up = jnp.dot(xn, wu_ref[...], preferred_element_type=F32)
    h = (_silu(gate) * up).astype(BF16)
    o_ref[...] += jnp.dot(h, wd_ref[...], preferred_element_type=F32)


def _ffn(x, g, wg, wu, wd, li, *, tm=512, th=512):
    m = x.shape[0]
    return pl.pallas_call(
        _ffn_kernel,
        grid=(m // tm, FFN_HIDDEN // th),
        in_specs=[
            pl.BlockSpec((tm, D_MODEL), lambda i, j: (i, 0)),
            pl.BlockSpec((1, D_MODEL), lambda i, j: (0, 0)),
            pl.BlockSpec((None, D_MODEL, th), lambda i, j: (li, 0, j)),
            pl.BlockSpec((None, D_MODEL, th), lambda i, j: (li, 0, j)),
            pl.BlockSpec((None, th, D_MODEL), lambda i, j: (li, j, 0)),
        ],
        out_specs=pl.BlockSpec((tm, D_MODEL), lambda i, j: (i, 0)),
        out_shape=jax.ShapeDtypeStruct((m, D_MODEL), F32),
        scratch_shapes=[pltpu.VMEM((tm, D_MODEL), BF16)],
        compiler_params=_cparams(("parallel", "arbitrary")),
        name="ffn",
    )(x, g, wg, wu, wd)


def _rope_tables(seq):
    pos = jnp.arange(seq, dtype=F32)
    inv_freq = ROPE_THETA ** (-jnp.arange(0, ROPE_DIM, 2, dtype=F32) / ROPE_DIM)
    ang = pos[:, None] * inv_freq[None, :]
    cos, sin = jnp.cos(ang), jnp.sin(ang)
    half = ROPE_DIM // 2
    zeros = lambda n: jnp.zeros((seq, n), F32)
    keep = jnp.concatenate([cos, cos, jnp.ones((seq, HEAD_DIM - ROPE_DIM), F32)], axis=-1)
    from_prev = jnp.concatenate([zeros(half), sin, zeros(HEAD_DIM - ROPE_DIM)], axis=-1)
    from_next = jnp.concatenate([-sin, zeros(HEAD_DIM - half)], axis=-1)
    return keep, from_prev, from_next


def _bf16_weights(p):
    w_in = p['w_in']
    v0, z0 = QK_COLS, QK_COLS + ATT_WIDTH
    return dict(
        w_qk=w_in[:, :, :v0].astype(BF16), w_v=w_in[:, :, v0:z0].astype(BF16),
        w_zx=w_in[:, :, z0:PROJ_COLS].astype(BF16),
        w_dt=jnp.pad(w_in[:, :, PROJ_COLS:], ((0, 0), (0, 0), (0, LANES - 2 * SSM_HEADS))).astype(BF16),
        w_att=p['w_out'][:, :ATT_WIDTH].astype(BF16), w_ssm=p['w_out'][:, ATT_WIDTH:].astype(BF16),
        pw1_w=p['pw1_w'].astype(BF16), pw2_w=p['pw2_w'].astype(BF16),
        w_gate=p['w_gate'].astype(BF16), w_up=p['w_up'].astype(BF16), w_down=p['w_down'].astype(BF16))


def _trunk(x3, p, w):
    n_seq, seq, _ = x3.shape
    x = x3.reshape(n_seq * seq, D_MODEL)
    rope = _rope_tables(seq)
    row = lambda v: v.reshape(1, -1)
    for layer in range(DEPTH):
        if layer % 2 == 0:
            e = layer // 2
            head_gain = jnp.concatenate([jnp.tile(p['q_norm'][e] * (HEAD_DIM ** -0.5), ATT_HEADS),
                                         jnp.tile(p['k_norm'][e], ATT_HEADS)]).reshape(1, QK_COLS)
            qk, v, zx, dt = _in_proj(x, row(p['mix_norm'][e]), w['w_qk'], w['w_v'], w['w_zx'], w['w_dt'], e,
                                     head_gain, rope, seq)
            att = _attention(qk, v, n_seq, seq)
            xbc = _dwconv(zx, p['ssm_conv_w'][e], p['ssm_conv_b'][e], n_seq, seq, SSM_WIDTH, SSM_CONV_CH, True, BF16)
            dtt = jnp.transpose(dt[:, :2 * SSM_HEADS])
            yf, yb = _ssd(xbc, dt, dtt, p['dt_bias'][e], p['a_log'][e], n_seq, seq)
            dskip = jnp.repeat(p['d_skip'][e], SSM_HEAD_DIM).reshape(1, SSM_WIDTH)
            x = _mix_out(att, yf, yb, xbc, zx, dskip, row(p['ssm_norm'][e]), w['w_att'], w['w_ssm'], e, x)
        else:
            o = layer // 2
            u = _pw1_glu(x, row(p['conf_norm'][o]), w['pw1_w'], o, row(p['pw1_b'][o]))
            u = _dwconv(u, p['dw_w'][o], p['dw_b'][o], n_seq, seq, 0, D_MODEL, False, F32)
            x = _pw2(u, row(p['ln_g'][o]), row(p['ln_b'][o]), w['pw2_w'], o, row(p['pw2_b'][o]), x)
        x = _ffn(x, row(p['ffn_norm'][layer]), w['w_gate'], w['w_up'], w['w_down'], layer)
    return x.reshape(n_seq, seq, D_MODEL)


def kernel(x_prompt, x_sample, mix_norm, w_in, q_norm, k_norm, ssm_conv_w, ssm_conv_b, a_log, dt_bias, d_skip, ssm_norm, w_out, conf_norm, pw1_w, pw1_b, dw_w, dw_b, ln_g, ln_b, pw2_w, pw2_b, ffn_norm, w_gate, w_up, w_down):
    params = dict(mix_norm=mix_norm, w_in=w_in, q_norm=q_norm, k_norm=k_norm, ssm_conv_w=ssm_conv_w,
                  ssm_conv_b=ssm_conv_b, a_log=a_log, dt_bias=dt_bias, d_skip=d_skip, ssm_norm=ssm_norm,
                  w_out=w_out, conf_norm=conf_norm, pw1_w=pw1_w, pw1_b=pw1_b, dw_w=dw_w, dw_b=dw_b,
                  ln_g=ln_g, ln_b=ln_b, pw2_w=pw2_w, pw2_b=pw2_b, ffn_norm=ffn_norm, w_gate=w_gate,
                  w_up=w_up, w_down=w_down)
    weights = _bf16_weights(params)
    return (_trunk(x_prompt, params, weights), _trunk(x_sample, params, weights))
```

```python
import functools
import math

import jax
import jax.numpy as jnp
from jax import lax
from jax.experimental import pallas as pl
from jax.experimental.pallas import tpu as pltpu

F32 = jnp.float32
BF16 = jnp.bfloat16

D_MODEL = 2048
DEPTH = 4
HEAD_DIM = 128
ATT_HEADS = 8
ATT_WIDTH = ATT_HEADS * HEAD_DIM
ROPE_DIM = HEAD_DIM // 4
ROPE_THETA = 500000.0
ATT_HALF = 64
SSM_HEAD_DIM = 64
SSM_WIDTH = 1024
SSM_HEADS = 16
SSM_GROUPS = 2
SSM_STATE = 128
SSM_CONV = 4
SSM_CHUNK = 128
SSM_CONV_CH = SSM_WIDTH + 2 * SSM_GROUPS * SSM_STATE
PROJ_COLS = 3 * ATT_WIDTH + SSM_WIDTH + SSM_CONV_CH
CONF_KERNEL = 31
FFN_HIDDEN = 5632
EPS = 1e-6
NEG_INF = -1e30

LANES = 128
VMEM_LIMIT = 48 * 1024 * 1024


def _cparams(sem):
    return pltpu.CompilerParams(dimension_semantics=sem, vmem_limit_bytes=VMEM_LIMIT)


def _rms_rows(x, g):
    ms = jnp.mean(x * x, axis=-1, keepdims=True)
    return x * lax.rsqrt(ms + EPS) * g


def _silu(x):
    return x * (1.0 / (1.0 + jnp.exp(-x)))


def _sigmoid(x):
    return 1.0 / (1.0 + jnp.exp(-x))


QK_COLS = 2 * ATT_WIDTH
ZX_COLS = SSM_WIDTH + SSM_CONV_CH


def _in_proj_kernel(x_ref, g_ref, wqk_ref, wv_ref, wzx_ref, wdt_ref, hg_ref, ra_ref, rp_ref, rm_ref,
                    oqk_ref, ov_ref, ozx_ref, odt_ref, xn_ref):
    @pl.when(pl.program_id(1) == 0)
    def _():
        xn = _rms_rows(x_ref[...], g_ref[...]).astype(BF16)
        xn_ref[...] = xn
        odt_ref[...] = jnp.dot(xn, wdt_ref[...], preferred_element_type=F32)

    xn = xn_ref[...]
    acc = jnp.dot(xn, wqk_ref[...], preferred_element_type=F32)
    ov_ref[...] = jnp.dot(xn, wv_ref[...], preferred_element_type=F32)
    ozx_ref[...] = jnp.dot(xn, wzx_ref[...], preferred_element_type=F32).astype(BF16)
    ra, rp, rm = ra_ref[...], rp_ref[...], rm_ref[...]
    for hh in range(acc.shape[1] // HEAD_DIM):
        sl = slice(hh * HEAD_DIM, (hh + 1) * HEAD_DIM)
        t = _rms_rows(acc[:, sl], hg_ref[:, sl])
        oqk_ref[:, sl] = (t * ra + pltpu.roll(t, ROPE_DIM // 2, 1) * rp
                          + pltpu.roll(t, HEAD_DIM - ROPE_DIM // 2, 1) * rm)


def _in_proj(x, g, w_in, w_zx, wdt, li, head_gain, rope, seq, *, tm=512, steps=4):
    m = x.shape[0]
    tq, tv, tz = QK_COLS // steps, ATT_WIDTH // steps, ZX_COLS // steps
    pos_tiles = seq // tm
    rope_spec = pl.BlockSpec((tm, HEAD_DIM), lambda i, j: (i % pos_tiles, 0))
    wspec = lambda t, col0: pl.BlockSpec((None, D_MODEL, t), lambda i, j: (li, 0, col0 // t + j))
    ospec = lambda t: pl.BlockSpec((tm, t), lambda i, j: (i, j))
    return pl.pallas_call(
        _in_proj_kernel,
        grid=(m // tm, steps),
        in_specs=[
            pl.BlockSpec((tm, D_MODEL), lambda i, j: (i, 0)),
            pl.BlockSpec((1, D_MODEL), lambda i, j: (0, 0)),
            wspec(tq, 0), wspec(tv, QK_COLS), wspec(tz, 0),
            pl.BlockSpec((None, D_MODEL, LANES), lambda i, j: (li, 0, 0)),
            pl.BlockSpec((1, tq), lambda i, j: (0, j)),
            rope_spec, rope_spec, rope_spec,
        ],
        out_specs=[ospec(tq), ospec(tv), ospec(tz), pl.BlockSpec((tm, LANES), lambda i, j: (i, 0))],
        out_shape=[jax.ShapeDtypeStruct((m, QK_COLS), F32), jax.ShapeDtypeStruct((m, ATT_WIDTH), F32),
                   jax.ShapeDtypeStruct((m, ZX_COLS), BF16), jax.ShapeDtypeStruct((m, LANES), F32)],
        scratch_shapes=[pltpu.VMEM((tm, D_MODEL), BF16)],
        compiler_params=_cparams(("parallel", "arbitrary")),
        name="in_proj",
    )(x, g, w_in, w_in, w_zx, wdt, head_gain, *rope)


ATT_BQ = 128
ATT_KW = ATT_BQ + 2 * ATT_HALF
ATT_UNROLL = 8


def _attn_branch(q, k, v, acc, den, mx, bias, cls_len, seq, first):
    bq = min(ATT_BQ, cls_len)
    kw = min(ATT_KW, cls_len)
    nb = cls_len // bq
    ones = jnp.ones((kw, HEAD_DIM), BF16)

    def load(t):
        r = t // nb
        b = t - r * nb
        kstart = jnp.clip(b * bq - ATT_HALF, 0, cls_len - kw)
        qoff = pl.multiple_of(r * cls_len + b * bq, ATT_HALF)
        koff = pl.multiple_of(r * cls_len + kstart, ATT_HALF)
        case = (b * bq - kstart) // ATT_HALF
        qv = q[pl.ds(qoff, bq), :].astype(BF16)
        kv = k[pl.ds(koff, kw), :].astype(BF16)
        vv = v[pl.ds(koff, kw), :].astype(BF16)
        bias_t = bias[case, :bq, :kw]
        old = None if first else (mx[pl.ds(qoff, bq), :], den[pl.ds(qoff, bq), :], acc[pl.ds(qoff, bq), :])
        return qoff, qv, kv, vv, bias_t, old

    def compute(qv, kv, vv, bias_t, old):
        s = lax.dot_general(qv, kv, (((1,), (1,)), ((), ())), preferred_element_type=F32) + bias_t
        m_new = jnp.broadcast_to(jnp.max(s, axis=-1, keepdims=True), (bq, HEAD_DIM))
        if old is not None:
            m_new = jnp.maximum(old[0], m_new)
        p = jnp.exp(s - jnp.concatenate([m_new] * (kw // HEAD_DIM), axis=1))
        pv = jnp.dot(p.astype(BF16), jnp.concatenate([vv, ones], axis=1), preferred_element_type=F32)
        num, rowsum = pv[:, :HEAD_DIM], pv[:, HEAD_DIM:]
        if old is not None:
            alpha = jnp.exp(old[0] - m_new)
            num = alpha * old[2] + num
            rowsum = alpha * old[1] + rowsum
        return m_new, rowsum, num

    def body(t, carry):
        loaded = [load(t * ATT_UNROLL + u) for u in range(ATT_UNROLL)]
        results = [compute(*item[1:]) for item in loaded]
        for (qoff, *_), (m_new, d_new, a_new) in zip(loaded, results):
            mx[pl.ds(qoff, bq), :] = m_new
            den[pl.ds(qoff, bq), :] = d_new
            acc[pl.ds(qoff, bq), :] = a_new
        return carry

    lax.fori_loop(0, seq // bq // ATT_UNROLL, body, 0)


def _regroup_by_4(srcs, dsts, seq, src_cls):
    sub = src_cls // 4
    ch = min(sub, 128)
    n_per = sub // ch
    n_old = seq // src_cls

    def body(t, carry):
        c = t // (4 * n_per)
        rem = t - c * (4 * n_per)
        a = rem // n_per
        i = rem - a * n_per
        dst_off = pl.multiple_of((a * n_old + c) * sub + i * ch, 8)
        src_off = c * src_cls + a + 4 * i * ch
        for s_ref, d_ref in zip(srcs, dsts):
            d_ref[pl.ds(dst_off, ch), :] = s_ref[pl.ds(src_off, ch, stride=4), :]
        return carry

    lax.fori_loop(0, n_old * 4 * n_per, body, 0, unroll=4)


def _attn_kernel(q_ref, k_ref, v_ref, o_ref, acc0, den0, mx0, q4, k4, v4, acc4, den4, mx4, q16, k16, v16, bias,
                 *, seq):
    dist = (lax.broadcasted_iota(jnp.int32, (ATT_BQ, ATT_KW), 0)
            - lax.broadcasted_iota(jnp.int32, (ATT_BQ, ATT_KW), 1))
    for case in range(3):
        bias[case] = jnp.where(jnp.abs(dist + case * ATT_HALF) <= ATT_HALF, 0.0, NEG_INF)
    _attn_branch(q_ref, k_ref, v_ref, acc0, den0, mx0, bias, seq, seq, True)
    _regroup_by_4((q_ref, k_ref, v_ref, acc0, den0, mx0), (q4, k4, v4, acc4, den4, mx4), seq, seq)
    _attn_branch(q4, k4, v4, acc4, den4, mx4, bias, seq // 4, seq, False)
    _regroup_by_4((q4, k4, v4, acc4, den4, mx4), (q16, k16, v16, acc0, den0, mx0), seq, seq // 4)
    _attn_branch(q16, k16, v16, acc0, den0, mx0, bias, seq // 16, seq, False)
    cls_len = seq // 16
    for r in range(16):
        rows = pl.ds(r * cls_len, cls_len)
        q4[pl.ds(r, cls_len, stride=16), :] = acc0[rows, :] * (1.0 / den0[rows, :])
    o_ref[...] = q4[...].astype(o_ref.dtype)


def _attention(qk, v, n_seq, seq):
    m = qk.shape[0]
    blk = lambda off: pl.BlockSpec((seq, HEAD_DIM), lambda b, h, off=off: (b, off + h))
    return pl.pallas_call(
        functools.partial(_attn_kernel, seq=seq),
        grid=(n_seq, ATT_HEADS),
        in_specs=[blk(0), blk(ATT_HEADS), blk(0)],
        out_specs=pl.BlockSpec((seq, HEAD_DIM), lambda b, h: (b, h)),
        out_shape=jax.ShapeDtypeStruct((m, ATT_WIDTH), BF16),
        scratch_shapes=([pltpu.VMEM((seq, HEAD_DIM), F32) for _ in range(12)]
                        + [pltpu.VMEM((3, ATT_BQ, ATT_KW), F32)]),
        compiler_params=_cparams(("parallel", "parallel")),
        name="dilated_attention",
    )(qk, qk, v)


CONV_PAD = 16
CONV_ROWS = 128


def _dwconv_kernel(x_ref, w_ref, b_ref, o_ref, pad_ref, *, seq, width, act):
    left = (width - 1) // 2
    zeros = jnp.zeros((CONV_PAD, LANES), F32)
    for slab in range(x_ref.shape[1] // LANES):
        lanes = slice(slab * LANES, (slab + 1) * LANES)
        pad = pad_ref.at[slab]
        pad[pl.ds(0, CONV_PAD), :] = zeros
        pad[pl.ds(CONV_PAD + seq, CONV_PAD), :] = zeros
        pad[pl.ds(CONV_PAD, seq), :] = x_ref[:, lanes].astype(F32)

        def body(i, carry, lanes=lanes, pad=pad):
            r0 = pl.multiple_of(i * CONV_ROWS, CONV_ROWS)
            acc = jnp.broadcast_to(b_ref[:, lanes], (CONV_ROWS, LANES))
            for tap in range(width):
                acc = acc + pad[pl.ds(r0 + CONV_PAD + tap - left, CONV_ROWS), :] * w_ref[tap:tap + 1, lanes]
            if act:
                acc = _silu(acc)
            o_ref[pl.ds(r0, CONV_ROWS), lanes] = acc.astype(o_ref.dtype)
            return carry

        lax.fori_loop(0, seq // CONV_ROWS, body, 0)


def _dwconv(x, w, b, n_seq, seq, col0, channels, act, out_dtype, cw):
    width = w.shape[0]
    cb0 = col0 // cw
    return pl.pallas_call(
        functools.partial(_dwconv_kernel, seq=seq, width=width, act=act),
        grid=(n_seq, channels // cw),
        in_specs=[
            pl.BlockSpec((seq, cw), lambda s, c: (s, cb0 + c)),
            pl.BlockSpec((width, cw), lambda s, c: (0, c)),
            pl.BlockSpec((1, cw), lambda s, c: (0, c)),
        ],
        out_specs=pl.BlockSpec((seq, cw), lambda s, c: (s, c)),
        out_shape=jax.ShapeDtypeStruct((n_seq * seq, channels), out_dtype),
        scratch_shapes=[pltpu.VMEM((cw // LANES, seq + 2 * CONV_PAD, LANES), F32)],
        compiler_params=_cparams(("parallel", "parallel")),
        name=f"dwconv{width}",
    )(x, w, b.reshape(1, channels))


def _split3(x):
    hi = x.astype(BF16)
    r1 = x - hi.astype(F32)
    mid = r1.astype(BF16)
    lo = (r1 - mid.astype(F32)).astype(BF16)
    return hi, mid, lo


def _dot_exact_rhs(a, b_bf16):
    return sum(jnp.dot(p, b_bf16, preferred_element_type=F32) for p in _split3(a))


def _dot_exact_lhs(a_bf16, b):
    return sum(jnp.dot(a_bf16, p, preferred_element_type=F32) for p in _split3(b))


def _softplus(x):
    return jnp.maximum(x, 0.0) + jnp.log(1.0 + jnp.exp(-jnp.abs(x)))


def _ssd_direction(xs_ref, b_ref, c_ref, dt_ref, dtt_ref, brow_ref, bcol_ref, arow_ref, acol_ref,
                   y_ref, h_ref, reverse):
    T = SSM_CHUNK
    d0 = SSM_HEADS if reverse else 0
    adt_r = _softplus(dt_ref[...] + brow_ref[...]) * (-jnp.exp(arow_ref[...]))
    delta_c = _softplus(dtt_ref[d0:d0 + SSM_HEADS, :] + bcol_ref[d0:d0 + SSM_HEADS, :])
    adt_c = delta_c * (-jnp.exp(acol_ref[d0:d0 + SSM_HEADS, :]))

    ri = lax.broadcasted_iota(jnp.int32, (T, T), 0)
    ci = lax.broadcasted_iota(jnp.int32, (T, T), 1)
    causal = (ci >= ri) if reverse else (ci <= ri)
    tri = jnp.where(causal, 1.0, 0.0).astype(BF16)
    tri_t = jnp.where((ri >= ci) if reverse else (ri <= ci), 1.0, 0.0).astype(BF16)
    cum_r = _dot_exact_lhs(tri, adt_r)
    cum_c = _dot_exact_rhs(adt_c, tri_t)
    end = 0 if reverse else T - 1
    total_c = cum_c[:, end:end + 1]
    src_w_c = delta_c * jnp.exp(total_c - cum_c)
    chunk_decay_c = jnp.exp(total_c)

    lane_lt64 = lax.broadcasted_iota(jnp.int32, (T, LANES), 1) < SSM_HEAD_DIM
    pairs_per_group = SSM_HEADS // SSM_GROUPS // 2

    for g in range(SSM_GROUPS):
        bm = b_ref[:, g * SSM_STATE:(g + 1) * SSM_STATE]
        cm = c_ref[:, g * SSM_STATE:(g + 1) * SSM_STATE]
        cb = lax.dot_general(cm, bm, (((1,), (1,)), ((), ())), preferred_element_type=F32)
        bt = jnp.transpose(bm.astype(F32))
        cm32 = cm.astype(F32)
        for kk in range(pairs_per_group):
            k = g * pairs_per_group + kk
            lanes = slice(k * LANES, (k + 1) * LANES)
            cols = slice(kk * LANES, (kk + 1) * LANES)
            xs = xs_ref[:, lanes]
            h_in = h_ref[g, :, cols]
            rhs = jnp.concatenate([xs, h_in.astype(BF16)], axis=0)
            ys, sts = [], []
            for hh in (2 * k, 2 * k + 1):
                a_l = jnp.broadcast_to(cum_r[:, d0 + hh:d0 + hh + 1], (T, T))
                decay = jnp.exp(jnp.where(causal, a_l - cum_c[hh:hh + 1, :], NEG_INF))
                gmat = cb * decay * delta_c[hh:hh + 1, :]
                lhs = jnp.concatenate([gmat, cm32 * jnp.exp(a_l)], axis=1).astype(BF16)
                ys.append(jnp.dot(lhs, rhs, preferred_element_type=F32))
                bts = (bt * src_w_c[hh:hh + 1, :]).astype(BF16)
                sts.append(jnp.dot(bts, xs, preferred_element_type=F32))
            y_ref[:, lanes] = jnp.where(lane_lt64, ys[0], ys[1]).astype(y_ref.dtype)
            cd = jnp.where(lane_lt64, chunk_decay_c[2 * k:2 * k + 1, :], chunk_decay_c[2 * k + 1:2 * k + 2, :])
            h_ref[g, :, cols] = h_in * cd + jnp.where(lane_lt64, sts[0], sts[1])


def _ssd_kernel(xsf, bf, cf, dtf, dttf, xsb, bb, cb, dtb, dttb, brow, bcol, arow, acol, yf, yb, hf, hb):
    @pl.when(pl.program_id(1) == 0)
    def _():
        hf[...] = jnp.zeros_like(hf)
        hb[...] = jnp.zeros_like(hb)

    _ssd_direction(xsf, bf, cf, dtf, dttf, brow, bcol, arow, acol, yf, hf, False)
    _ssd_direction(xsb, bb, cb, dtb, dttb, brow, bcol, arow, acol, yb, hb, True)


def _ssd(xbc, dt, dtt, dt_bias, a_log, n_seq, seq):
    m = xbc.shape[0]
    T = SSM_CHUNK
    nc = seq // T
    fwd = lambda s, i: s * nc + i
    bwd = lambda s, i: s * nc + (nc - 1 - i)
    gn = SSM_GROUPS * SSM_STATE

    def specs(row):
        return [
            pl.BlockSpec((T, SSM_WIDTH), lambda s, i: (row(s, i), 0)),
            pl.BlockSpec((T, gn), lambda s, i: (row(s, i), SSM_WIDTH // gn)),
            pl.BlockSpec((T, gn), lambda s, i: (row(s, i), SSM_WIDTH // gn + 1)),
            pl.BlockSpec((T, LANES), lambda s, i: (row(s, i), 0)),
            pl.BlockSpec((2 * SSM_HEADS, T), lambda s, i: (0, row(s, i))),
        ]

    small = lambda shape: pl.BlockSpec(shape, lambda s, i: (0, 0))
    nh2 = 2 * SSM_HEADS
    lane_row = lambda v: jnp.pad(v.reshape(1, nh2), ((0, 0), (0, LANES - nh2)))
    return pl.pallas_call(
        _ssd_kernel,
        grid=(n_seq, nc),
        in_specs=specs(fwd) + specs(bwd) + [small((1, LANES)), small((nh2, 1)), small((1, LANES)), small((nh2, 1))],
        out_specs=[
            pl.BlockSpec((T, SSM_WIDTH), lambda s, i: (fwd(s, i), 0)),
            pl.BlockSpec((T, SSM_WIDTH), lambda s, i: (bwd(s, i), 0)),
        ],
        out_shape=[jax.ShapeDtypeStruct((m, SSM_WIDTH), BF16)] * 2,
        scratch_shapes=[pltpu.VMEM((SSM_GROUPS, SSM_STATE, SSM_WIDTH // SSM_GROUPS), F32)] * 2,
        compiler_params=_cparams(("parallel", "arbitrary")),
        name="ssd",
    )(xbc, xbc, xbc, dt, dtt, xbc, xbc, xbc, dt, dtt,
      lane_row(dt_bias), dt_bias.reshape(nh2, 1), lane_row(a_log), a_log.reshape(nh2, 1))


ROW_CHUNK = 128


def _mix_out_kernel(att_ref, yf_ref, yb_ref, xs_ref, z_ref, dsk_ref, gn_ref, wa_ref, ws_ref, x_ref, o_ref):
    gw = SSM_WIDTH // SSM_GROUPS
    for c in range(x_ref.shape[0] // ROW_CHUNK):
        rows = slice(c * ROW_CHUNK, (c + 1) * ROW_CHUNK)
        y = ((yf_ref[rows, :].astype(F32) + yb_ref[rows, :].astype(F32) + xs_ref[rows, :].astype(F32) * dsk_ref[...])
             * _silu(z_ref[rows, :].astype(F32)))
        ssm = jnp.concatenate([_rms_rows(y[:, g * gw:(g + 1) * gw], gn_ref[:, g * gw:(g + 1) * gw])
                               for g in range(SSM_GROUPS)], axis=1).astype(BF16)
        o_ref[rows, :] = (x_ref[rows, :]
                          + jnp.dot(att_ref[rows, :], wa_ref[...], preferred_element_type=F32)
                          + jnp.dot(ssm, ws_ref[...], preferred_element_type=F32))


def _mix_out(att, yf, yb, xbc, zx, dskip, gnorm, w_out, li, x, *, tm=512):
    m = x.shape[0]
    row = pl.BlockSpec((tm, SSM_WIDTH), lambda i: (i, 0))
    vec = pl.BlockSpec((1, SSM_WIDTH), lambda i: (0, 0))
    wsp = lambda half: pl.BlockSpec((None, SSM_WIDTH, D_MODEL), lambda i: (li, half, 0))
    full = pl.BlockSpec((tm, D_MODEL), lambda i: (i, 0))
    return pl.pallas_call(
        _mix_out_kernel,
        grid=(m // tm,),
        in_specs=[row, row, row, row, row, vec, vec, wsp(0), wsp(1), full],
        out_specs=full,
        out_shape=jax.ShapeDtypeStruct((m, D_MODEL), F32),
        compiler_params=_cparams(("parallel",)),
        name="mix_out",
    )(att, yf, yb, xbc, zx, dskip, gnorm, w_out, w_out, x)


def _pw1_glu_kernel(x_ref, g_ref, w_ref, b_ref, o_ref):
    for c in range(x_ref.shape[0] // ROW_CHUNK):
        rows = slice(c * ROW_CHUNK, (c + 1) * ROW_CHUNK)
        xn = _rms_rows(x_ref[rows, :], g_ref[...]).astype(BF16)
        a = jnp.dot(xn, w_ref[:, :D_MODEL], preferred_element_type=F32) + b_ref[:, :D_MODEL]
        gate = jnp.dot(xn, w_ref[:, D_MODEL:], preferred_element_type=F32) + b_ref[:, D_MODEL:]
        o_ref[rows, :] = (a * _sigmoid(gate)).astype(o_ref.dtype)


def _pw1_glu(x, g, w, li, b, *, tm=256):
    m = x.shape[0]
    full = pl.BlockSpec((tm, D_MODEL), lambda i: (i, 0))
    return pl.pallas_call(
        _pw1_glu_kernel,
        grid=(m // tm,),
        in_specs=[
            full,
            pl.BlockSpec((1, D_MODEL), lambda i: (0, 0)),
            pl.BlockSpec((None, D_MODEL, 2 * D_MODEL), lambda i: (li, 0, 0)),
            pl.BlockSpec((1, 2 * D_MODEL), lambda i: (0, 0)),
        ],
        out_specs=full,
        out_shape=jax.ShapeDtypeStruct((m, D_MODEL), F32),
        compiler_params=_cparams(("parallel",)),
        name="pw1_glu",
    )(x, g, w, b)


def _pw2_kernel(u_ref, lg_ref, lb_ref, w_ref, b_ref, x_ref, o_ref):
    for c in range(x_ref.shape[0] // ROW_CHUNK):
        rows = slice(c * ROW_CHUNK, (c + 1) * ROW_CHUNK)
        u = u_ref[rows, :]
        mu = jnp.mean(u, axis=-1, keepdims=True)
        uc = u - mu
        var = jnp.mean(uc * uc, axis=-1, keepdims=True)
        un = _silu(uc * lax.rsqrt(var + EPS) * lg_ref[...] + lb_ref[...]).astype(BF16)
        o_ref[rows, :] = x_ref[rows, :] + b_ref[...] + jnp.dot(un, w_ref[...], preferred_element_type=F32)


def _pw2(u, ln_g, ln_b, w, li, b, x, *, tm=512):
    m = x.shape[0]
    vec = pl.BlockSpec((1, D_MODEL), lambda i: (0, 0))
    full = pl.BlockSpec((tm, D_MODEL), lambda i: (i, 0))
    return pl.pallas_call(
        _pw2_kernel,
        grid=(m // tm,),
        in_specs=[full, vec, vec, pl.BlockSpec((None, D_MODEL, D_MODEL), lambda i: (li, 0, 0)), vec, full],
        out_specs=full,
        out_shape=jax.ShapeDtypeStruct((m, D_MODEL), F32),
        compiler_params=_cparams(("parallel",)),
        name="pw2",
    )(u, ln_g, ln_b, w, b, x)


def _ffn_kernel(x_ref, g_ref, wg_ref, wu_ref, wd_ref, o_ref, xn_ref):
    @pl.when(pl.program_id(1) == 0)
    def _():
        x = x_ref[...]
        xn_ref[...] = _rms_rows(x, g_ref[...]).astype(BF16)
        o_ref[...] = x

    xn = xn_ref[...]
    gate = jnp.dot(xn, wg_ref[...], preferred_element_type=F32)
    up = jnp.dot(xn, wu_ref[...], preferred_element_type=F32)
    h = (_silu(gate) * up).astype(BF16)
    o_ref[...] += jnp.dot(h, wd_ref[...], preferred_element_type=F32)


def _ffn(x, g, wg, wu, wd, li, *, tm=1024, th=256):
    m = x.shape[0]
    return pl.pallas_call(
        _ffn_kernel,
        grid=(m // tm, FFN_HIDDEN // th),
        in_specs=[
            pl.BlockSpec((tm, D_MODEL), lambda i, j: (i, 0)),
            pl.BlockSpec((1, D_MODEL), lambda i, j: (0, 0)),
            pl.BlockSpec((None, D_MODEL, th), lambda i, j: (li, 0, j)),
            pl.BlockSpec((None, D_MODEL, th), lambda i, j: (li, 0, j)),
            pl.BlockSpec((None, th, D_MODEL), lambda i, j: (li, j, 0)),
        ],
        out_specs=pl.BlockSpec((tm, D_MODEL), lambda i, j: (i, 0)),
        out_shape=jax.ShapeDtypeStruct((m, D_MODEL), F32),
        scratch_shapes=[pltpu.VMEM((tm, D_MODEL), BF16)],
        compiler_params=_cparams(("parallel", "arbitrary")),
        name="ffn",
    )(x, g, wg, wu, wd)


def _rope_tables(seq):
    pos = jnp.arange(seq, dtype=F32)
    inv_freq = ROPE_THETA ** (-jnp.arange(0, ROPE_DIM, 2, dtype=F32) / ROPE_DIM)
    ang = pos[:, None] * inv_freq[None, :]
    cos, sin = jnp.cos(ang), jnp.sin(ang)
    half = ROPE_DIM // 2
    zeros = lambda n: jnp.zeros((seq, n), F32)
    keep = jnp.concatenate([cos, cos, jnp.ones((seq, HEAD_DIM - ROPE_DIM), F32)], axis=-1)
    from_prev = jnp.concatenate([zeros(half), sin, zeros(HEAD_DIM - ROPE_DIM)], axis=-1)
    from_next = jnp.concatenate([-sin, zeros(HEAD_DIM - half)], axis=-1)
    return keep, from_prev, from_next


def _bf16_weights(p):
    w_in = p['w_in'].astype(BF16)
    return dict(
        w_in=w_in, w_zx=w_in[:, :, QK_COLS + ATT_WIDTH:PROJ_COLS],
        w_dt=jnp.pad(w_in[:, :, PROJ_COLS:], ((0, 0), (0, 0), (0, LANES - 2 * SSM_HEADS))),
        w_out=p['w_out'].astype(BF16), pw1_w=p['pw1_w'].astype(BF16), pw2_w=p['pw2_w'].astype(BF16),
        w_gate=p['w_gate'].astype(BF16), w_up=p['w_up'].astype(BF16), w_down=p['w_down'].astype(BF16))


def _trunk(x3, p, w):
    n_seq, seq, _ = x3.shape
    x = x3.reshape(n_seq * seq, D_MODEL)
    rope = _rope_tables(seq)
    row = lambda v: v.reshape(1, -1)
    for layer in range(DEPTH):
        if layer % 2 == 0:
            e = layer // 2
            head_gain = jnp.concatenate([jnp.tile(p['q_norm'][e] * (HEAD_DIM ** -0.5), ATT_HEADS),
                                         jnp.tile(p['k_norm'][e], ATT_HEADS)]).reshape(1, QK_COLS)
            qk, v, zx, dt = _in_proj(x, row(p['mix_norm'][e]), w['w_in'], w['w_zx'], w['w_dt'], e,
                                     head_gain, rope, seq)
            att = _attention(qk, v, n_seq, seq)
            xbc = _dwconv(zx, p['ssm_conv_w'][e], p['ssm_conv_b'][e], n_seq, seq, SSM_WIDTH, SSM_CONV_CH, True, BF16, 512)
            dtt = jnp.transpose(dt[:, :2 * SSM_HEADS])
            yf, yb = _ssd(xbc, dt, dtt, p['dt_bias'][e], p['a_log'][e], n_seq, seq)
            dskip = jnp.repeat(p['d_skip'][e], SSM_HEAD_DIM).reshape(1, SSM_WIDTH)
            x = _mix_out(att, yf, yb, xbc, zx, dskip, row(p['ssm_norm'][e]), w['w_out'], e, x)
        else:
            o = layer // 2
            u = _pw1_glu(x, row(p['conf_norm'][o]), w['pw1_w'], o, row(p['pw1_b'][o]))
            u = _dwconv(u, p['dw_w'][o], p['dw_b'][o], n_seq, seq, 0, D_MODEL, False, F32, 256)
            x = _pw2(u, row(p['ln_g'][o]), row(p['ln_b'][o]), w['pw2_w'], o, row(p['pw2_b'][o]), x)
        x = _ffn(x, row(p['ffn_norm'][layer]), w['w_gate'], w['w_up'], w['w_down'], layer)
    return x.reshape(n_seq, seq, D_MODEL)


def kernel(x_prompt, x_sample, mix_norm, w_in, q_norm, k_norm, ssm_conv_w, ssm_conv_b, a_log, dt_bias, d_skip, ssm_norm, w_out, conf_norm, pw1_w, pw1_b, dw_w, dw_b, ln_g, ln_b, pw2_w, pw2_b, ffn_norm, w_gate, w_up, w_down):
    params = dict(mix_norm=mix_norm, w_in=w_in, q_norm=q_norm, k_norm=k_norm, ssm_conv_w=ssm_conv_w,
                  ssm_conv_b=ssm_conv_b, a_log=a_log, dt_bias=dt_bias, d_skip=d_skip, ssm_norm=ssm_norm,
                  w_out=w_out, conf_norm=conf_norm, pw1_w=pw1_w, pw1_b=pw1_b, dw_w=dw_w, dw_b=dw_b,
                  ln_g=ln_g, ln_b=ln_b, pw2_w=pw2_w, pw2_b=pw2_b, ffn_norm=ffn_norm, w_gate=w_gate,
                  w_up=w_up, w_down=w_down)
    weights = _bf16_weights(params)
    return (_trunk(x_prompt, params, weights), _trunk(x_sample, params, weights))
```

```python
import functools
import math

import jax
import jax.numpy as jnp
from jax import lax
from jax.experimental import pallas as pl
from jax.experimental.pallas import tpu as pltpu

F32 = jnp.float32
BF16 = jnp.bfloat16

D_MODEL = 2048
DEPTH = 4
HEAD_DIM = 128
ATT_HEADS = 8
ATT_WIDTH = ATT_HEADS * HEAD_DIM
ROPE_DIM = HEAD_DIM // 4
ROPE_THETA = 500000.0
ATT_HALF = 64
SSM_HEAD_DIM = 64
SSM_WIDTH = 1024
SSM_HEADS = 16
SSM_GROUPS = 2
SSM_STATE = 128
SSM_CONV = 4
SSM_CHUNK = 128
SSM_CONV_CH = SSM_WIDTH + 2 * SSM_GROUPS * SSM_STATE
PROJ_COLS = 3 * ATT_WIDTH + SSM_WIDTH + SSM_CONV_CH
CONF_KERNEL = 31
FFN_HIDDEN = 5632
EPS = 1e-6
NEG_INF = -1e30

LANES = 128
VMEM_LIMIT = 48 * 1024 * 1024


def _cparams(sem):
    return pltpu.CompilerParams(dimension_semantics=sem, vmem_limit_bytes=VMEM_LIMIT)


def _rms_rows(x, g):
    ms = jnp.mean(x * x, axis=-1, keepdims=True)
    return x * lax.rsqrt(ms + EPS) * g


def _silu(x):
    return x * (1.0 / (1.0 + jnp.exp(-x)))


def _sigmoid(x):
    return 1.0 / (1.0 + jnp.exp(-x))


QK_COLS = 2 * ATT_WIDTH
ZX_COLS = SSM_WIDTH + SSM_CONV_CH


def _in_proj_kernel(x_ref, g_ref, wqk_ref, wv_ref, wzx_ref, wdt_ref, hg_ref, ra_ref, rp_ref, rm_ref,
                    oqk_ref, ov_ref, ozx_ref, odt_ref, xn_ref):
    @pl.when(pl.program_id(1) == 0)
    def _():
        xn = _rms_rows(x_ref[...], g_ref[...]).astype(BF16)
        xn_ref[...] = xn
        odt_ref[...] = jnp.dot(xn, wdt_ref[...], preferred_element_type=F32)

    xn = xn_ref[...]
    acc = jnp.dot(xn, wqk_ref[...], preferred_element_type=F32)
    ov_ref[...] = jnp.dot(xn, wv_ref[...], preferred_element_type=F32)
    ozx_ref[...] = jnp.dot(xn, wzx_ref[...], preferred_element_type=F32).astype(BF16)
    ra, rp, rm = ra_ref[...], rp_ref[...], rm_ref[...]
    for hh in range(acc.shape[1] // HEAD_DIM):
        sl = slice(hh * HEAD_DIM, (hh + 1) * HEAD_DIM)
        t = _rms_rows(acc[:, sl], hg_ref[:, sl])
        oqk_ref[:, sl] = (t * ra + pltpu.roll(t, ROPE_DIM // 2, 1) * rp
                          + pltpu.roll(t, HEAD_DIM - ROPE_DIM // 2, 1) * rm)


def _in_proj(x, g, w_in, w_zx, wdt, li, head_gain, rope, seq, *, tm=512, steps=4):
    m = x.shape[0]
    tq, tv, tz = QK_COLS // steps, ATT_WIDTH // steps, ZX_COLS // steps
    pos_tiles = seq // tm
    rope_spec = pl.BlockSpec((tm, HEAD_DIM), lambda i, j: (i % pos_tiles, 0))
    wspec = lambda t, col0: pl.BlockSpec((None, D_MODEL, t), lambda i, j: (li, 0, col0 // t + j))
    ospec = lambda t: pl.BlockSpec((tm, t), lambda i, j: (i, j))
    return pl.pallas_call(
        _in_proj_kernel,
        grid=(m // tm, steps),
        in_specs=[
            pl.BlockSpec((tm, D_MODEL), lambda i, j: (i, 0)),
            pl.BlockSpec((1, D_MODEL), lambda i, j: (0, 0)),
            wspec(tq, 0), wspec(tv, QK_COLS), wspec(tz, 0),
            pl.BlockSpec((None, D_MODEL, LANES), lambda i, j: (li, 0, 0)),
            pl.BlockSpec((1, tq), lambda i, j: (0, j)),
            rope_spec, rope_spec, rope_spec,
        ],
        out_specs=[ospec(tq), ospec(tv), ospec(tz), pl.BlockSpec((tm, LANES), lambda i, j: (i, 0))],
        out_shape=[jax.ShapeDtypeStruct((m, QK_COLS), F32), jax.ShapeDtypeStruct((m, ATT_WIDTH), F32),
                   jax.ShapeDtypeStruct((m, ZX_COLS), BF16), jax.ShapeDtypeStruct((m, LANES), F32)],
        scratch_shapes=[pltpu.VMEM((tm, D_MODEL), BF16)],
        compiler_params=_cparams(("parallel", "arbitrary")),
        name="in_proj",
    )(x, g, w_in, w_in, w_zx, wdt, head_gain, *rope)


ATT_BQ = 128
ATT_KW = ATT_BQ + 2 * ATT_HALF
ATT_UNROLL = 16


def _attn_branch(q, k, v, acc, den, mx, bias, cls_len, seq, first):
    bq = min(ATT_BQ, cls_len)
    kw = min(ATT_KW, cls_len)
    nb = cls_len // bq
    ones = jnp.ones((kw, HEAD_DIM), BF16)

    def load(t):
        r = t // nb
        b = t - r * nb
        kstart = jnp.clip(b * bq - ATT_HALF, 0, cls_len - kw)
        qoff = pl.multiple_of(r * cls_len + b * bq, ATT_HALF)
        koff = pl.multiple_of(r * cls_len + kstart, ATT_HALF)
        case = (b * bq - kstart) // ATT_HALF
        qv = q[pl.ds(qoff, bq), :].astype(BF16)
        kv = k[pl.ds(koff, kw), :].astype(BF16)
        vv = v[pl.ds(koff, kw), :].astype(BF16)
        bias_t = bias[case, :bq, :kw]
        old = None if first else (mx[pl.ds(qoff, bq), :], den[pl.ds(qoff, bq), :], acc[pl.ds(qoff, bq), :])
        return qoff, qv, kv, vv, bias_t, old

    def compute(qv, kv, vv, bias_t, old):
        s = lax.dot_general(qv, kv, (((1,), (1,)), ((), ())), preferred_element_type=F32) + bias_t
        m_new = jnp.broadcast_to(jnp.max(s, axis=-1, keepdims=True), (bq, HEAD_DIM))
        if old is not None:
            m_new = jnp.maximum(old[0], m_new)
        p = jnp.exp2(s - jnp.concatenate([m_new] * (kw // HEAD_DIM), axis=1))
        pv = jnp.dot(p.astype(BF16), jnp.concatenate([vv, ones], axis=1), preferred_element_type=F32)
        num, rowsum = pv[:, :HEAD_DIM], pv[:, HEAD_DIM:]
        if old is not None:
            alpha = jnp.exp2(old[0] - m_new)
            num = alpha * old[2] + num
            rowsum = alpha * old[1] + rowsum
        return m_new, rowsum, num

    def body(t, carry):
        loaded = [load(t * ATT_UNROLL + u) for u in range(ATT_UNROLL)]
        results = [compute(*item[1:]) for item in loaded]
        for (qoff, *_), (m_new, d_new, a_new) in zip(loaded, results):
            mx[pl.ds(qoff, bq), :] = m_new
            den[pl.ds(qoff, bq), :] = d_new
            acc[pl.ds(qoff, bq), :] = a_new
        return carry

    lax.fori_loop(0, seq // bq // ATT_UNROLL, body, 0)


def _regroup_by_4(srcs, dsts, seq, src_cls):
    sub = src_cls // 4
    ch = min(sub, 128)
    n_per = sub // ch
    n_old = seq // src_cls

    def body(t, carry):
        c = t // (4 * n_per)
        rem = t - c * (4 * n_per)
        a = rem // n_per
        i = rem - a * n_per
        dst_off = pl.multiple_of((a * n_old + c) * sub + i * ch, ch)
        src_off = c * src_cls + a + 4 * i * ch
        for s_ref, d_refs in zip(srcs, dsts):
            val = s_ref[pl.ds(src_off, ch, stride=4), :]
            for d_ref in d_refs:
                d_ref[pl.ds(dst_off, ch), :] = val.astype(d_ref.dtype)
        return carry

    lax.fori_loop(0, n_old * 4 * n_per, body, 0, unroll=4)


def _attn_kernel(q_ref, k_ref, v_ref, o_ref, acc0, den0, mx0, q4, k4, v4, acc4, den4, mx4,
                 k4b, v4b, q16b, k16b, v16b, bias, *, seq):
    dist = (lax.broadcasted_iota(jnp.int32, (ATT_BQ, ATT_KW), 0)
            - lax.broadcasted_iota(jnp.int32, (ATT_BQ, ATT_KW), 1))
    for case in range(3):
        bias[case] = jnp.where(jnp.abs(dist + case * ATT_HALF) <= ATT_HALF, 0.0, NEG_INF)
    _attn_branch(q_ref, k_ref, v_ref, acc0, den0, mx0, bias, seq, seq, True)
    _regroup_by_4((q_ref, k_ref, v_ref, acc0, den0, mx0),
                  ((q4,), (k4, k4b), (v4, v4b), (acc4,), (den4,), (mx4,)), seq, seq)
    _attn_branch(q4, k4b, v4b, acc4, den4, mx4, bias, seq // 4, seq, False)
    _regroup_by_4((q4, k4, v4, acc4, den4, mx4),
                  ((q16b,), (k16b,), (v16b,), (acc0,), (den0,), (mx0,)), seq, seq // 4)
    _attn_branch(q16b, k16b, v16b, acc0, den0, mx0, bias, seq // 16, seq, False)
    cls_len = seq // 16
    for r in range(16):
        rows = pl.ds(r * cls_len, cls_len)
        q4[pl.ds(r, cls_len, stride=16), :] = acc0[rows, :] * (1.0 / den0[rows, :])
    o_ref[...] = q4[...].astype(o_ref.dtype)


def _attention(qk, v, n_seq, seq):
    m = qk.shape[0]
    blk = lambda off: pl.BlockSpec((seq, HEAD_DIM), lambda b, h, off=off: (b, off + h))
    return pl.pallas_call(
        functools.partial(_attn_kernel, seq=seq),
        grid=(n_seq, ATT_HEADS),
        in_specs=[blk(0), blk(ATT_HEADS), blk(0)],
        out_specs=pl.BlockSpec((seq, HEAD_DIM), lambda b, h: (b, h)),
        out_shape=jax.ShapeDtypeStruct((m, ATT_WIDTH), BF16),
        scratch_shapes=([pltpu.VMEM((seq, HEAD_DIM), F32) for _ in range(9)]
                        + [pltpu.VMEM((seq, HEAD_DIM), BF16) for _ in range(5)]
                        + [pltpu.VMEM((3, ATT_BQ, ATT_KW), F32)]),
        compiler_params=_cparams(("parallel", "parallel")),
        name="dilated_attention",
    )(qk, qk, v)


CONV_PAD = 16
CONV_ROWS = 128


def _dwconv_kernel(x_ref, w_ref, b_ref, o_ref, pad_ref, *, seq, width, act):
    left = (width - 1) // 2
    zeros = jnp.zeros((CONV_PAD, LANES), F32)
    for slab in range(x_ref.shape[1] // LANES):
        lanes = slice(slab * LANES, (slab + 1) * LANES)
        pad = pad_ref.at[slab]
        pad[pl.ds(0, CONV_PAD), :] = zeros
        pad[pl.ds(CONV_PAD + seq, CONV_PAD), :] = zeros
        pad[pl.ds(CONV_PAD, seq), :] = x_ref[:, lanes].astype(F32)

        def body(i, carry, lanes=lanes, pad=pad):
            r0 = pl.multiple_of(i * CONV_ROWS, CONV_ROWS)
            acc = jnp.broadcast_to(b_ref[:, lanes], (CONV_ROWS, LANES))
            for tap in range(width):
                acc = acc + pad[pl.ds(r0 + CONV_PAD + tap - left, CONV_ROWS), :] * w_ref[tap:tap + 1, lanes]
            if act:
                acc = _silu(acc)
            o_ref[pl.ds(r0, CONV_ROWS), lanes] = acc.astype(o_ref.dtype)
            return carry

        lax.fori_loop(0, seq // CONV_ROWS, body, 0)


def _dwconv(x, w, b, n_seq, seq, col0, channels, act, out_dtype, cw):
    width = w.shape[0]
    cb0 = col0 // cw
    return pl.pallas_call(
        functools.partial(_dwconv_kernel, seq=seq, width=width, act=act),
        grid=(n_seq, channels // cw),
        in_specs=[
            pl.BlockSpec((seq, cw), lambda s, c: (s, cb0 + c)),
            pl.BlockSpec((width, cw), lambda s, c: (0, c)),
            pl.BlockSpec((1, cw), lambda s, c: (0, c)),
        ],
        out_specs=pl.BlockSpec((seq, cw), lambda s, c: (s, c)),
        out_shape=jax.ShapeDtypeStruct((n_seq * seq, channels), out_dtype),
        scratch_shapes=[pltpu.VMEM((cw // LANES, seq + 2 * CONV_PAD, LANES), F32)],
        compiler_params=_cparams(("parallel", "parallel")),
        name=f"dwconv{width}",
    )(x, w, b.reshape(1, channels))


def _split3(x):
    hi = x.astype(BF16)
    r1 = x - hi.astype(F32)
    mid = r1.astype(BF16)
    lo = (r1 - mid.astype(F32)).astype(BF16)
    return hi, mid, lo


def _dot_exact_rhs(a, b_bf16):
    return sum(jnp.dot(p, b_bf16, preferred_element_type=F32) for p in _split3(a))


def _dot_exact_lhs(a_bf16, b):
    return sum(jnp.dot(a_bf16, p, preferred_element_type=F32) for p in _split3(b))


def _softplus(x):
    return jnp.maximum(x, 0.0) + jnp.log(1.0 + jnp.exp(-jnp.abs(x)))


def _ssd_direction(xs_ref, b_ref, c_ref, dt_ref, dtt_ref, brow_ref, bcol_ref, arow_ref, acol_ref,
                   y_ref, h_ref, reverse):
    T = SSM_CHUNK
    d0 = SSM_HEADS if reverse else 0
    log2e = math.log2(math.e)
    adt_r = _softplus(dt_ref[...] + brow_ref[...]) * (-log2e * jnp.exp(arow_ref[...]))
    delta_c = _softplus(dtt_ref[d0:d0 + SSM_HEADS, :] + bcol_ref[d0:d0 + SSM_HEADS, :])
    adt_c = delta_c * (-log2e * jnp.exp(acol_ref[d0:d0 + SSM_HEADS, :]))

    ri = lax.broadcasted_iota(jnp.int32, (T, T), 0)
    ci = lax.broadcasted_iota(jnp.int32, (T, T), 1)
    causal = (ci >= ri) if reverse else (ci <= ri)
    tri = jnp.where(causal, 1.0, 0.0).astype(BF16)
    tri_t = jnp.where((ri >= ci) if reverse else (ri <= ci), 1.0, 0.0).astype(BF16)
    cum_r = _dot_exact_lhs(tri, adt_r)
    cum_c = _dot_exact_rhs(adt_c, tri_t)
    end = 0 if reverse else T - 1
    total_c = cum_c[:, end:end + 1]
    src_w_c = delta_c * jnp.exp2(total_c - cum_c)
    chunk_decay_c = jnp.exp2(total_c)

    lane_lt64 = lax.broadcasted_iota(jnp.int32, (T, LANES), 1) < SSM_HEAD_DIM
    pairs_per_group = SSM_HEADS // SSM_GROUPS // 2

    for g in range(SSM_GROUPS):
        bm = b_ref[:, g * SSM_STATE:(g + 1) * SSM_STATE]
        cm = c_ref[:, g * SSM_STATE:(g + 1) * SSM_STATE]
        cb = lax.dot_general(cm, bm, (((1,), (1,)), ((), ())), preferred_element_type=F32)
        bt = jnp.transpose(bm.astype(F32))
        cm32 = cm.astype(F32)
        for kk in range(pairs_per_group):
            k = g * pairs_per_group + kk
            lanes = slice(k * LANES, (k + 1) * LANES)
            cols = slice(kk * LANES, (kk + 1) * LANES)
            xs = xs_ref[:, lanes]
            h_in = h_ref[g, :, cols]
            rhs = jnp.concatenate([xs, h_in.astype(BF16)], axis=0)
            ys, sts = [], []
            for hh in (2 * k, 2 * k + 1):
                a_l = jnp.broadcast_to(cum_r[:, d0 + hh:d0 + hh + 1], (T, T))
                decay = jnp.exp2(jnp.where(causal, a_l - cum_c[hh:hh + 1, :], NEG_INF))
                gmat = cb * decay * delta_c[hh:hh + 1, :]
                lhs = jnp.concatenate([gmat, cm32 * jnp.exp2(a_l)], axis=1).astype(BF16)
                ys.append(jnp.dot(lhs, rhs, preferred_element_type=F32))
                bts = (bt * src_w_c[hh:hh + 1, :]).astype(BF16)
                sts.append(jnp.dot(bts, xs, preferred_element_type=F32))
            y_ref[:, lanes] = jnp.where(lane_lt64, ys[0], ys[1]).astype(y_ref.dtype)
            cd = jnp.where(lane_lt64, chunk_decay_c[2 * k:2 * k + 1, :], chunk_decay_c[2 * k + 1:2 * k + 2, :])
            h_ref[g, :, cols] = h_in * cd + jnp.where(lane_lt64, sts[0], sts[1])


def _ssd_kernel(xsf, bf, cf, dtf, dttf, xsb, bb, cb, dtb, dttb, brow, bcol, arow, acol, yf, yb, hf, hb):
    @pl.when(pl.program_id(1) == 0)
    def _():
        hf[...] = jnp.zeros_like(hf)
        hb[...] = jnp.zeros_like(hb)

    _ssd_direction(xsf, bf, cf, dtf, dttf, brow, bcol, arow, acol, yf, hf, False)
    _ssd_direction(xsb, bb, cb, dtb, dttb, brow, bcol, arow, acol, yb, hb, True)


def _ssd(xbc, dt, dtt, dt_bias, a_log, n_seq, seq):
    m = xbc.shape[0]
    T = SSM_CHUNK
    nc = seq // T
    fwd = lambda s, i: s * nc + i
    bwd = lambda s, i: s * nc + (nc - 1 - i)
    gn = SSM_GROUPS * SSM_STATE

    def specs(row):
        return [
            pl.BlockSpec((T, SSM_WIDTH), lambda s, i: (row(s, i), 0)),
            pl.BlockSpec((T, gn), lambda s, i: (row(s, i), SSM_WIDTH // gn)),
            pl.BlockSpec((T, gn), lambda s, i: (row(s, i), SSM_WIDTH // gn + 1)),
            pl.BlockSpec((T, LANES), lambda s, i: (row(s, i), 0)),
            pl.BlockSpec((2 * SSM_HEADS, T), lambda s, i: (0, row(s, i))),
        ]

    small = lambda shape: pl.BlockSpec(shape, lambda s, i: (0, 0))
    nh2 = 2 * SSM_HEADS
    lane_row = lambda v: jnp.pad(v.reshape(1, nh2), ((0, 0), (0, LANES - nh2)))
    return pl.pallas_call(
        _ssd_kernel,
        grid=(n_seq, nc),
        in_specs=specs(fwd) + specs(bwd) + [small((1, LANES)), small((nh2, 1)), small((1, LANES)), small((nh2, 1))],
        out_specs=[
            pl.BlockSpec((T, SSM_WIDTH), lambda s, i: (fwd(s, i), 0)),
            pl.BlockSpec((T, SSM_WIDTH), lambda s, i: (bwd(s, i), 0)),
        ],
        out_shape=[jax.ShapeDtypeStruct((m, SSM_WIDTH), BF16)] * 2,
        scratch_shapes=[pltpu.VMEM((SSM_GROUPS, SSM_STATE, SSM_WIDTH // SSM_GROUPS), F32)] * 2,
        compiler_params=_cparams(("parallel", "arbitrary")),
        name="ssd",
    )(xbc, xbc, xbc, dt, dtt, xbc, xbc, xbc, dt, dtt,
      lane_row(dt_bias), dt_bias.reshape(nh2, 1), lane_row(a_log), a_log.reshape(nh2, 1))


ROW_CHUNK = 128


def _mix_out_kernel(att_ref, yf_ref, yb_ref, xs_ref, z_ref, dsk_ref, gn_ref, wa_ref, ws_ref, x_ref, o_ref):
    gw = SSM_WIDTH // SSM_GROUPS
    for c in range(x_ref.shape[0] // ROW_CHUNK):
        rows = slice(c * ROW_CHUNK, (c + 1) * ROW_CHUNK)
        y = ((yf_ref[rows, :].astype(F32) + yb_ref[rows, :].astype(F32) + xs_ref[rows, :].astype(F32) * dsk_ref[...])
             * _silu(z_ref[rows, :].astype(F32)))
        ssm = jnp.concatenate([_rms_rows(y[:, g * gw:(g + 1) * gw], gn_ref[:, g * gw:(g + 1) * gw])
                               for g in range(SSM_GROUPS)], axis=1).astype(BF16)
        o_ref[rows, :] = (x_ref[rows, :]
                          + jnp.dot(att_ref[rows, :], wa_ref[...], preferred_element_type=F32)
                          + jnp.dot(ssm, ws_ref[...], preferred_element_type=F32))


def _mix_out(att, yf, yb, xbc, zx, dskip, gnorm, w_out, li, x, *, tm=512):
    m = x.shape[0]
    row = pl.BlockSpec((tm, SSM_WIDTH), lambda i: (i, 0))
    vec = pl.BlockSpec((1, SSM_WIDTH), lambda i: (0, 0))
    wsp = lambda half: pl.BlockSpec((None, SSM_WIDTH, D_MODEL), lambda i: (li, half, 0))
    full = pl.BlockSpec((tm, D_MODEL), lambda i: (i, 0))
    return pl.pallas_call(
        _mix_out_kernel,
        grid=(m // tm,),
        in_specs=[row, row, row, row, row, vec, vec, wsp(0), wsp(1), full],
        out_specs=full,
        out_shape=jax.ShapeDtypeStruct((m, D_MODEL), F32),
        compiler_params=_cparams(("parallel",)),
        name="mix_out",
    )(att, yf, yb, xbc, zx, dskip, gnorm, w_out, w_out, x)


def _pw1_glu_kernel(x_ref, g_ref, w_ref, b_ref, o_ref):
    for c in range(x_ref.shape[0] // ROW_CHUNK):
        rows = slice(c * ROW_CHUNK, (c + 1) * ROW_CHUNK)
        xn = _rms_rows(x_ref[rows, :], g_ref[...]).astype(BF16)
        a = jnp.dot(xn, w_ref[:, :D_MODEL], preferred_element_type=F32) + b_ref[:, :D_MODEL]
        gate = jnp.dot(xn, w_ref[:, D_MODEL:], preferred_element_type=F32) + b_ref[:, D_MODEL:]
        o_ref[rows, :] = (a * _sigmoid(gate)).astype(o_ref.dtype)


def _pw1_glu(x, g, w, li, b, *, tm=256):
    m = x.shape[0]
    full = pl.BlockSpec((tm, D_MODEL), lambda i: (i, 0))
    return pl.pallas_call(
        _pw1_glu_kernel,
        grid=(m // tm,),
        in_specs=[
            full,
            pl.BlockSpec((1, D_MODEL), lambda i: (0, 0)),
            pl.BlockSpec((None, D_MODEL, 2 * D_MODEL), lambda i: (li, 0, 0)),
            pl.BlockSpec((1, 2 * D_MODEL), lambda i: (0, 0)),
        ],
        out_specs=full,
        out_shape=jax.ShapeDtypeStruct((m, D_MODEL), F32),
        compiler_params=_cparams(("parallel",)),
        name="pw1_glu",
    )(x, g, w, b)


def _pw2_kernel(u_ref, lg_ref, lb_ref, w_ref, b_ref, x_ref, o_ref):
    for c in range(x_ref.shape[0] // ROW_CHUNK):
        rows = slice(c * ROW_CHUNK, (c + 1) * ROW_CHUNK)
        u = u_ref[rows, :]
        mu = jnp.mean(u, axis=-1, keepdims=True)
        uc = u - mu
        var = jnp.mean(uc * uc, axis=-1, keepdims=True)
        un = _silu(uc * lax.rsqrt(var + EPS) * lg_ref[...] + lb_ref[...]).astype(BF16)
        o_ref[rows, :] = x_ref[rows, :] + b_ref[...] + jnp.dot(un, w_ref[...], preferred_element_type=F32)


def _pw2(u, ln_g, ln_b, w, li, b, x, *, tm=512):
    m = x.shape[0]
    vec = pl.BlockSpec((1, D_MODEL), lambda i: (0, 0))
    full = pl.BlockSpec((tm, D_MODEL), lambda i: (i, 0))
    return pl.pallas_call(
        _pw2_kernel,
        grid=(m // tm,),
        in_specs=[full, vec, vec, pl.BlockSpec((None, D_MODEL, D_MODEL), lambda i: (li, 0, 0)), vec, full],
        out_specs=full,
        out_shape=jax.ShapeDtypeStruct((m, D_MODEL), F32),
        compiler_params=_cparams(("parallel",)),
        name="pw2",
    )(u, ln_g, ln_b, w, b, x)


def _ffn_kernel(x_ref, g_ref, wg_ref, wu_ref, wd_ref, o_ref, xn_ref):
    @pl.when(pl.program_id(1) == 0)
    def _():
        x = x_ref[...]
        xn_ref[...] = _rms_rows(x, g_ref[...]).astype(BF16)
        o_ref[...] = x

    xn = xn_ref[...]
    gate = jnp.dot(xn, wg_ref[...], preferred_element_type=F32)
    up = jnp.dot(xn, wu_ref[...], preferred_element_type=F32)
    h = (_silu(gate) * up).astype(BF16)
    o_ref[...] += jnp.dot(h, wd_ref[...], preferred_element_type=F32)


def _ffn(x, g, wg, wu, wd, li, *, tm=1024, th=256):
    m = x.shape[0]
    return pl.pallas_call(
        _ffn_kernel,
        grid=(m // tm, FFN_HIDDEN // th),
        in_specs=[
            pl.BlockSpec((tm, D_MODEL), lambda i, j: (i, 0)),
            pl.BlockSpec((1, D_MODEL), lambda i, j: (0, 0)),
            pl.BlockSpec((None, D_MODEL, th), lambda i, j: (li, 0, j)),
            pl.BlockSpec((None, D_MODEL, th), lambda i, j: (li, 0, j)),
            pl.BlockSpec((None, th, D_MODEL), lambda i, j: (li, j, 0)),
        ],
        out_specs=pl.BlockSpec((tm, D_MODEL), lambda i, j: (i, 0)),
        out_shape=jax.ShapeDtypeStruct((m, D_MODEL), F32),
        scratch_shapes=[pltpu.VMEM((tm, D_MODEL), BF16)],
        compiler_params=_cparams(("parallel", "arbitrary")),
        name="ffn",
    )(x, g, wg, wu, wd)


def _rope_tables(seq):
    pos = jnp.arange(seq, dtype=F32)
    inv_freq = ROPE_THETA ** (-jnp.arange(0, ROPE_DIM, 2, dtype=F32) / ROPE_DIM)
    ang = pos[:, None] * inv_freq[None, :]
    cos, sin = jnp.cos(ang), jnp.sin(ang)
    half = ROPE_DIM // 2
    zeros = lambda n: jnp.zeros((seq, n), F32)
    keep = jnp.concatenate([cos, cos, jnp.ones((seq, HEAD_DIM - ROPE_DIM), F32)], axis=-1)
    from_prev = jnp.concatenate([zeros(half), sin, zeros(HEAD_DIM - ROPE_DIM)], axis=-1)
    from_next = jnp.concatenate([-sin, zeros(HEAD_DIM - half)], axis=-1)
    return keep, from_prev, from_next


def _bf16_weights(p):
    w_in = p['w_in']
    return dict(
        w_in=w_in[:, :, :PROJ_COLS].astype(BF16),
        w_zx=w_in[:, :, QK_COLS + ATT_WIDTH:PROJ_COLS].astype(BF16),
        w_dt=jnp.pad(w_in[:, :, PROJ_COLS:], ((0, 0), (0, 0), (0, LANES - 2 * SSM_HEADS))).astype(BF16),
        w_out=p['w_out'].astype(BF16), pw1_w=p['pw1_w'].astype(BF16), pw2_w=p['pw2_w'].astype(BF16),
        w_gate=p['w_gate'].astype(BF16), w_up=p['w_up'].astype(BF16), w_down=p['w_down'].astype(BF16))


def _trunk(x3, p, w):
    n_seq, seq, _ = x3.shape
    x = x3.reshape(n_seq * seq, D_MODEL)
    rope = _rope_tables(seq)
    row = lambda v: v.reshape(1, -1)
    for layer in range(DEPTH):
        if layer % 2 == 0:
            e = layer // 2
            head_gain = jnp.concatenate([jnp.tile(p['q_norm'][e] * (HEAD_DIM ** -0.5 * math.log2(math.e)), ATT_HEADS),
                                         jnp.tile(p['k_norm'][e], ATT_HEADS)]).reshape(1, QK_COLS)
            qk, v, zx, dt = _in_proj(x, row(p['mix_norm'][e]), w['w_in'], w['w_zx'], w['w_dt'], e,
                                     head_gain, rope, seq)
            att = _attention(qk, v, n_seq, seq)
            xbc = _dwconv(zx, p['ssm_conv_w'][e], p['ssm_conv_b'][e], n_seq, seq, SSM_WIDTH, SSM_CONV_CH, True, BF16, 512)
            dtt = jnp.transpose(dt[:, :2 * SSM_HEADS])
            yf, yb = _ssd(xbc, dt, dtt, p['dt_bias'][e], p['a_log'][e], n_seq, seq)
            dskip = jnp.repeat(p['d_skip'][e], SSM_HEAD_DIM).reshape(1, SSM_WIDTH)
            x = _mix_out(att, yf, yb, xbc, zx, dskip, row(p['ssm_norm'][e]), w['w_out'], e, x)
        else:
            o = layer // 2
            u = _pw1_glu(x, row(p['conf_norm'][o]), w['pw1_w'], o, row(p['pw1_b'][o]))
            u = _dwconv(u, p['dw_w'][o], p['dw_b'][o], n_seq, seq, 0, D_MODEL, False, F32, 256)
            x = _pw2(u, row(p['ln_g'][o]), row(p['ln_b'][o]), w['pw2_w'], o, row(p['pw2_b'][o]), x)
        x = _ffn(x, row(p['ffn_norm'][layer]), w['w_gate'], w['w_up'], w['w_down'], layer)
    return x.reshape(n_seq, seq, D_MODEL)


def kernel(x_prompt, x_sample, mix_norm, w_in, q_norm, k_norm, ssm_conv_w, ssm_conv_b, a_log, dt_bias, d_skip, ssm_norm, w_out, conf_norm, pw1_w, pw1_b, dw_w, dw_b, ln_g, ln_b, pw2_w, pw2_b, ffn_norm, w_gate, w_up, w_down):
    params = dict(mix_norm=mix_norm, w_in=w_in, q_norm=q_norm, k_norm=k_norm, ssm_conv_w=ssm_conv_w,
                  ssm_conv_b=ssm_conv_b, a_log=a_log, dt_bias=dt_bias, d_skip=d_skip, ssm_norm=ssm_norm,
                  w_out=w_out, conf_norm=conf_norm, pw1_w=pw1_w, pw1_b=pw1_b, dw_w=dw_w, dw_b=dw_b,
                  ln_g=ln_g, ln_b=ln_b, pw2_w=pw2_w, pw2_b=pw2_b, ffn_norm=ffn_norm, w_gate=w_gate,
                  w_up=w_up, w_down=w_down)
    weights = _bf16_weights(params)
    return (_trunk(x_prompt, params, weights), _trunk(x_sample, params, weights))
```

```python
import functools
import math

import jax
import jax.numpy as jnp
from jax import lax
from jax.experimental import pallas as pl
from jax.experimental.pallas import tpu as pltpu

F32 = jnp.float32
BF16 = jnp.bfloat16

D_MODEL = 2048
DEPTH = 4
HEAD_DIM = 128
ATT_HEADS = 8
ATT_WIDTH = ATT_HEADS * HEAD_DIM
ROPE_DIM = HEAD_DIM // 4
ROPE_THETA = 500000.0
ATT_HALF = 64
SSM_HEAD_DIM = 64
SSM_WIDTH = 1024
SSM_HEADS = 16
SSM_GROUPS = 2
SSM_STATE = 128
SSM_CONV = 4
SSM_CHUNK = 128
SSM_CONV_CH = SSM_WIDTH + 2 * SSM_GROUPS * SSM_STATE
PROJ_COLS = 3 * ATT_WIDTH + SSM_WIDTH + SSM_CONV_CH
CONF_KERNEL = 31
FFN_HIDDEN = 5632
EPS = 1e-6
NEG_INF = -1e30

LANES = 128
VMEM_LIMIT = 48 * 1024 * 1024


def _cparams(sem):
    return pltpu.CompilerParams(dimension_semantics=sem, vmem_limit_bytes=VMEM_LIMIT)


def _rms_rows(x, g):
    ms = jnp.mean(x * x, axis=-1, keepdims=True)
    return x * lax.rsqrt(ms + EPS) * g


def _silu(x):
    return x * (1.0 / (1.0 + jnp.exp(-x)))


def _sigmoid(x):
    return 1.0 / (1.0 + jnp.exp(-x))


QK_COLS = 2 * ATT_WIDTH
ZX_COLS = SSM_WIDTH + SSM_CONV_CH


def _in_proj_kernel(x_ref, g_ref, wqk_ref, wv_ref, wzx_ref, wdt_ref, hg_ref, ra_ref, rp_ref, rm_ref,
                    oqk_ref, ov_ref, ozx_ref, odt_ref, xn_ref):
    @pl.when(pl.program_id(1) == 0)
    def _():
        xn = _rms_rows(x_ref[...], g_ref[...]).astype(BF16)
        xn_ref[...] = xn
        odt_ref[...] = jnp.dot(xn, wdt_ref[...], preferred_element_type=F32)

    xn = xn_ref[...]
    acc = jnp.dot(xn, wqk_ref[...], preferred_element_type=F32)
    ov_ref[...] = jnp.dot(xn, wv_ref[...], preferred_element_type=F32)
    ozx_ref[...] = jnp.dot(xn, wzx_ref[...], preferred_element_type=F32).astype(BF16)
    ra, rp, rm = ra_ref[...], rp_ref[...], rm_ref[...]
    for hh in range(acc.shape[1] // HEAD_DIM):
        sl = slice(hh * HEAD_DIM, (hh + 1) * HEAD_DIM)
        t = _rms_rows(acc[:, sl], hg_ref[:, sl])
        oqk_ref[:, sl] = (t * ra + pltpu.roll(t, ROPE_DIM // 2, 1) * rp
                          + pltpu.roll(t, HEAD_DIM - ROPE_DIM // 2, 1) * rm)


def _in_proj(x, g, w_in, w_zx, wdt, li, head_gain, rope, seq, *, tm=512, steps=4):
    m = x.shape[0]
    tq, tv, tz = QK_COLS // steps, ATT_WIDTH // steps, ZX_COLS // steps
    pos_tiles = seq // tm
    rope_spec = pl.BlockSpec((tm, HEAD_DIM), lambda i, j: (i % pos_tiles, 0))
    wspec = lambda t, col0: pl.BlockSpec((None, D_MODEL, t), lambda i, j: (li, 0, col0 // t + j))
    ospec = lambda t: pl.BlockSpec((tm, t), lambda i, j: (i, j))
    return pl.pallas_call(
        _in_proj_kernel,
        grid=(m // tm, steps),
        in_specs=[
            pl.BlockSpec((tm, D_MODEL), lambda i, j: (i, 0)),
            pl.BlockSpec((1, D_MODEL), lambda i, j: (0, 0)),
            wspec(tq, 0), wspec(tv, QK_COLS), wspec(tz, 0),
            pl.BlockSpec((None, D_MODEL, LANES), lambda i, j: (li, 0, 0)),
            pl.BlockSpec((1, tq), lambda i, j: (0, j)),
            rope_spec, rope_spec, rope_spec,
        ],
        out_specs=[ospec(tq), ospec(tv), ospec(tz), pl.BlockSpec((tm, LANES), lambda i, j: (i, 0))],
        out_shape=[jax.ShapeDtypeStruct((m, QK_COLS), F32), jax.ShapeDtypeStruct((m, ATT_WIDTH), F32),
                   jax.ShapeDtypeStruct((m, ZX_COLS), BF16), jax.ShapeDtypeStruct((m, LANES), F32)],
        scratch_shapes=[pltpu.VMEM((tm, D_MODEL), BF16)],
        compiler_params=_cparams(("parallel", "arbitrary")),
        name="in_proj",
    )(x, g, w_in, w_in, w_zx, wdt, head_gain, *rope)


ATT_BQ = 128
ATT_KW = ATT_BQ + 2 * ATT_HALF
ATT_UNROLL = 16


def _attn_branch(q, k, v, state, prev_state, bias, cls_len, seq):
    acc, den, mx = state
    bq = min(ATT_BQ, cls_len)
    kw = min(ATT_KW, cls_len)
    nb = cls_len // bq
    n_prev = seq // (4 * cls_len)
    ones = jnp.ones((kw, HEAD_DIM), BF16)

    def load(t):
        r = t // nb
        b = t - r * nb
        kstart = jnp.clip(b * bq - ATT_HALF, 0, cls_len - kw)
        qoff = pl.multiple_of(r * cls_len + b * bq, ATT_HALF)
        koff = pl.multiple_of(r * cls_len + kstart, ATT_HALF)
        case = (b * bq - kstart) // ATT_HALF
        qv = q[pl.ds(qoff, bq), :].astype(BF16)
        kv = k[pl.ds(koff, kw), :].astype(BF16)
        vv = v[pl.ds(koff, kw), :].astype(BF16)
        bias_t = bias[case, :bq, :kw]
        old = None
        if prev_state is not None:
            a = r // n_prev
            c = r - a * n_prev
            prev_rows = pl.ds(c * (4 * cls_len) + a + 4 * b * bq, bq, stride=4)
            old = tuple(ref[prev_rows, :] for ref in (prev_state[2], prev_state[1], prev_state[0]))
        return qoff, qv, kv, vv, bias_t, old

    def compute(qv, kv, vv, bias_t, old):
        s = lax.dot_general(qv, kv, (((1,), (1,)), ((), ())), preferred_element_type=F32) + bias_t
        m_new = jnp.broadcast_to(jnp.max(s, axis=-1, keepdims=True), (bq, HEAD_DIM))
        if old is not None:
            m_new = jnp.maximum(old[0], m_new)
        p = jnp.exp2(s - jnp.concatenate([m_new] * (kw // HEAD_DIM), axis=1))
        pv = jnp.dot(p.astype(BF16), jnp.concatenate([vv, ones], axis=1), preferred_element_type=F32)
        num, rowsum = pv[:, :HEAD_DIM], pv[:, HEAD_DIM:]
        if old is not None:
            alpha = jnp.exp2(old[0] - m_new)
            num = alpha * old[2] + num
            rowsum = alpha * old[1] + rowsum
        return m_new, rowsum, num

    def body(t, carry):
        loaded = [load(t * ATT_UNROLL + u) for u in range(ATT_UNROLL)]
        results = [compute(*item[1:]) for item in loaded]
        for (qoff, *_), (m_new, d_new, a_new) in zip(loaded, results):
            mx[pl.ds(qoff, bq), :] = m_new
            den[pl.ds(qoff, bq), :] = d_new
            acc[pl.ds(qoff, bq), :] = a_new
        return carry

    lax.fori_loop(0, seq // bq // ATT_UNROLL, body, 0)


def _regroup_by_4(srcs, dsts, seq, src_cls):
    sub = src_cls // 4
    ch = min(sub, 128)
    n_per = sub // ch
    n_old = seq // src_cls

    def body(t, carry):
        c = t // (4 * n_per)
        rem = t - c * (4 * n_per)
        a = rem // n_per
        i = rem - a * n_per
        dst_off = pl.multiple_of((a * n_old + c) * sub + i * ch, ch)
        src_off = c * src_cls + a + 4 * i * ch
        for s_ref, d_refs in zip(srcs, dsts):
            val = s_ref[pl.ds(src_off, ch, stride=4), :]
            for d_ref in d_refs:
                d_ref[pl.ds(dst_off, ch), :] = val.astype(d_ref.dtype)
        return carry

    lax.fori_loop(0, n_old * 4 * n_per, body, 0, unroll=4)


def _attn_kernel(q_ref, k_ref, v_ref, o_ref, acc0, den0, mx0, q4, k4, v4, acc4, den4, mx4,
                 k4b, v4b, q16b, k16b, v16b, bias, *, seq):
    dist = (lax.broadcasted_iota(jnp.int32, (ATT_BQ, ATT_KW), 0)
            - lax.broadcasted_iota(jnp.int32, (ATT_BQ, ATT_KW), 1))
    for case in range(3):
        bias[case] = jnp.where(jnp.abs(dist + case * ATT_HALF) <= ATT_HALF, 0.0, NEG_INF)
    state0, state4 = (acc0, den0, mx0), (acc4, den4, mx4)
    _attn_branch(q_ref, k_ref, v_ref, state0, None, bias, seq, seq)
    _regroup_by_4((q_ref, k_ref, v_ref), ((q4,), (k4, k4b), (v4, v4b)), seq, seq)
    _attn_branch(q4, k4b, v4b, state4, state0, bias, seq // 4, seq)
    _regroup_by_4((q4, k4, v4), ((q16b,), (k16b,), (v16b,)), seq, seq // 4)
    _attn_branch(q16b, k16b, v16b, state0, state4, bias, seq // 16, seq)
    cls_len = seq // 16
    for r in range(16):
        rows = pl.ds(r * cls_len, cls_len)
        q4[pl.ds(r, cls_len, stride=16), :] = acc0[rows, :] * (1.0 / den0[rows, :])
    o_ref[...] = q4[...].astype(o_ref.dtype)


def _attention(qk, v, n_seq, seq):
    m = qk.shape[0]
    blk = lambda off: pl.BlockSpec((seq, HEAD_DIM), lambda b, h, off=off: (b, off + h))
    return pl.pallas_call(
        functools.partial(_attn_kernel, seq=seq),
        grid=(n_seq, ATT_HEADS),
        in_specs=[blk(0), blk(ATT_HEADS), blk(0)],
        out_specs=pl.BlockSpec((seq, HEAD_DIM), lambda b, h: (b, h)),
        out_shape=jax.ShapeDtypeStruct((m, ATT_WIDTH), BF16),
        scratch_shapes=([pltpu.VMEM((seq, HEAD_DIM), F32) for _ in range(9)]
                        + [pltpu.VMEM((seq, HEAD_DIM), BF16) for _ in range(5)]
                        + [pltpu.VMEM((3, ATT_BQ, ATT_KW), F32)]),
        compiler_params=_cparams(("parallel", "parallel")),
        name="dilated_attention",
    )(qk, qk, v)


CONV_PAD = 16
CONV_ROWS = 128


def _dwconv_kernel(x_ref, w_ref, b_ref, o_ref, pad_ref, *, seq, width, act):
    left = (width - 1) // 2
    zeros = jnp.zeros((CONV_PAD, LANES), F32)
    for slab in range(x_ref.shape[1] // LANES):
        lanes = slice(slab * LANES, (slab + 1) * LANES)
        pad = pad_ref.at[slab]
        pad[pl.ds(0, CONV_PAD), :] = zeros
        pad[pl.ds(CONV_PAD + seq, CONV_PAD), :] = zeros
        pad[pl.ds(CONV_PAD, seq), :] = x_ref[:, lanes].astype(F32)

        def body(i, carry, lanes=lanes, pad=pad):
            r0 = pl.multiple_of(i * CONV_ROWS, CONV_ROWS)
            acc = jnp.broadcast_to(b_ref[:, lanes], (CONV_ROWS, LANES))
            for tap in range(width):
                acc = acc + pad[pl.ds(r0 + CONV_PAD + tap - left, CONV_ROWS), :] * w_ref[tap:tap + 1, lanes]
            if act:
                acc = _silu(acc)
            o_ref[pl.ds(r0, CONV_ROWS), lanes] = acc.astype(o_ref.dtype)
            return carry

        lax.fori_loop(0, seq // CONV_ROWS, body, 0)


def _dwconv(x, w, b, n_seq, seq, col0, channels, act, out_dtype, cw):
    width = w.shape[0]
    cb0 = col0 // cw
    return pl.pallas_call(
        functools.partial(_dwconv_kernel, seq=seq, width=width, act=act),
        grid=(n_seq, channels // cw),
        in_specs=[
            pl.BlockSpec((seq, cw), lambda s, c: (s, cb0 + c)),
            pl.BlockSpec((width, cw), lambda s, c: (0, c)),
            pl.BlockSpec((1, cw), lambda s, c: (0, c)),
        ],
        out_specs=pl.BlockSpec((seq, cw), lambda s, c: (s, c)),
        out_shape=jax.ShapeDtypeStruct((n_seq * seq, channels), out_dtype),
        scratch_shapes=[pltpu.VMEM((cw // LANES, seq + 2 * CONV_PAD, LANES), F32)],
        compiler_params=_cparams(("parallel", "parallel")),
        name=f"dwconv{width}",
    )(x, w, b.reshape(1, channels))


def _split3(x):
    hi = x.astype(BF16)
    r1 = x - hi.astype(F32)
    mid = r1.astype(BF16)
    lo = (r1 - mid.astype(F32)).astype(BF16)
    return hi, mid, lo


def _dot_exact_rhs(a, b_bf16):
    return sum(jnp.dot(p, b_bf16, preferred_element_type=F32) for p in _split3(a))


def _dot_exact_lhs(a_bf16, b):
    return sum(jnp.dot(a_bf16, p, preferred_element_type=F32) for p in _split3(b))


def _softplus(x):
    return jnp.maximum(x, 0.0) + jnp.log(1.0 + jnp.exp(-jnp.abs(x)))


def _ssd_direction(xs_ref, b_ref, c_ref, dt_ref, dtt_ref, brow_ref, bcol_ref, arow_ref, acol_ref,
                   y_ref, h_ref, reverse):
    T = SSM_CHUNK
    d0 = SSM_HEADS if reverse else 0
    log2e = math.log2(math.e)
    adt_r = _softplus(dt_ref[...] + brow_ref[...]) * (-log2e * jnp.exp(arow_ref[...]))
    delta_c = _softplus(dtt_ref[d0:d0 + SSM_HEADS, :] + bcol_ref[d0:d0 + SSM_HEADS, :])
    adt_c = delta_c * (-log2e * jnp.exp(acol_ref[d0:d0 + SSM_HEADS, :]))

    ri = lax.broadcasted_iota(jnp.int32, (T, T), 0)
    ci = lax.broadcasted_iota(jnp.int32, (T, T), 1)
    causal = (ci >= ri) if reverse else (ci <= ri)
    tri = jnp.where(causal, 1.0, 0.0).astype(BF16)
    tri_t = jnp.where((ri >= ci) if reverse else (ri <= ci), 1.0, 0.0).astype(BF16)
    cum_r = _dot_exact_lhs(tri, adt_r)
    cum_c = _dot_exact_rhs(adt_c, tri_t)
    end = 0 if reverse else T - 1
    total_c = cum_c[:, end:end + 1]
    src_w_c = delta_c * jnp.exp2(total_c - cum_c)
    chunk_decay_c = jnp.exp2(total_c)

    lane_lt64 = lax.broadcasted_iota(jnp.int32, (T, LANES), 1) < SSM_HEAD_DIM
    pairs_per_group = SSM_HEADS // SSM_GROUPS // 2

    for g in range(SSM_GROUPS):
        bm = b_ref[:, g * SSM_STATE:(g + 1) * SSM_STATE]
        cm = c_ref[:, g * SSM_STATE:(g + 1) * SSM_STATE]
        cb = lax.dot_general(cm, bm, (((1,), (1,)), ((), ())), preferred_element_type=F32)
        bt = jnp.transpose(bm.astype(F32))
        cm32 = cm.astype(F32)
        for kk in range(pairs_per_group):
            k = g * pairs_per_group + kk
            lanes = slice(k * LANES, (k + 1) * LANES)
            cols = slice(kk * LANES, (kk + 1) * LANES)
            xs = xs_ref[:, lanes]
            h_in = h_ref[g, :, cols]
            rhs = jnp.concatenate([xs, h_in.astype(BF16)], axis=0)
            ys, sts = [], []
            for hh in (2 * k, 2 * k + 1):
                a_l = jnp.broadcast_to(cum_r[:, d0 + hh:d0 + hh + 1], (T, T))
                decay = jnp.exp2(jnp.where(causal, a_l - cum_c[hh:hh + 1, :], NEG_INF))
                gmat = cb * decay * delta_c[hh:hh + 1, :]
                lhs = jnp.concatenate([gmat, cm32 * jnp.exp2(a_l)], axis=1).astype(BF16)
                ys.append(jnp.dot(lhs, rhs, preferred_element_type=F32))
                bts = (bt * src_w_c[hh:hh + 1, :]).astype(BF16)
                sts.append(jnp.dot(bts, xs, preferred_element_type=F32))
            y_ref[:, lanes] = jnp.where(lane_lt64, ys[0], ys[1]).astype(y_ref.dtype)
            cd = jnp.where(lane_lt64, chunk_decay_c[2 * k:2 * k + 1, :], chunk_decay_c[2 * k + 1:2 * k + 2, :])
            h_ref[g, :, cols] = h_in * cd + jnp.where(lane_lt64, sts[0], sts[1])


def _ssd_kernel(xsf, bf, cf, dtf, dttf, xsb, bb, cb, dtb, dttb, brow, bcol, arow, acol, yf, yb, hf, hb):
    @pl.when(pl.program_id(1) == 0)
    def _():
        hf[...] = jnp.zeros_like(hf)
        hb[...] = jnp.zeros_like(hb)

    _ssd_direction(xsf, bf, cf, dtf, dttf, brow, bcol, arow, acol, yf, hf, False)
    _ssd_direction(xsb, bb, cb, dtb, dttb, brow, bcol, arow, acol, yb, hb, True)


def _ssd(xbc, dt, dtt, dt_bias, a_log, n_seq, seq):
    m = xbc.shape[0]
    T = SSM_CHUNK
    nc = seq // T
    fwd = lambda s, i: s * nc + i
    bwd = lambda s, i: s * nc + (nc - 1 - i)
    gn = SSM_GROUPS * SSM_STATE

    def specs(row):
        return [
            pl.BlockSpec((T, SSM_WIDTH), lambda s, i: (row(s, i), 0)),
            pl.BlockSpec((T, gn), lambda s, i: (row(s, i), SSM_WIDTH // gn)),
            pl.BlockSpec((T, gn), lambda s, i: (row(s, i), SSM_WIDTH // gn + 1)),
            pl.BlockSpec((T, LANES), lambda s, i: (row(s, i), 0)),
            pl.BlockSpec((2 * SSM_HEADS, T), lambda s, i: (0, row(s, i))),
        ]

    small = lambda shape: pl.BlockSpec(shape, lambda s, i: (0, 0))
    nh2 = 2 * SSM_HEADS
    lane_row = lambda v: jnp.pad(v.reshape(1, nh2), ((0, 0), (0, LANES - nh2)))
    return pl.pallas_call(
        _ssd_kernel,
        grid=(n_seq, nc),
        in_specs=specs(fwd) + specs(bwd) + [small((1, LANES)), small((nh2, 1)), small((1, LANES)), small((nh2, 1))],
        out_specs=[
            pl.BlockSpec((T, SSM_WIDTH), lambda s, i: (fwd(s, i), 0)),
            pl.BlockSpec((T, SSM_WIDTH), lambda s, i: (bwd(s, i), 0)),
        ],
        out_shape=[jax.ShapeDtypeStruct((m, SSM_WIDTH), BF16)] * 2,
        scratch_shapes=[pltpu.VMEM((SSM_GROUPS, SSM_STATE, SSM_WIDTH // SSM_GROUPS), F32)] * 2,
        compiler_params=_cparams(("parallel", "arbitrary")),
        name="ssd",
    )(xbc, xbc, xbc, dt, dtt, xbc, xbc, xbc, dt, dtt,
      lane_row(dt_bias), dt_bias.reshape(nh2, 1), lane_row(a_log), a_log.reshape(nh2, 1))


ROW_CHUNK = 128


def _mix_out_kernel(att_ref, yf_ref, yb_ref, xs_ref, z_ref, dsk_ref, gn_ref, wa_ref, ws_ref, x_ref, o_ref):
    gw = SSM_WIDTH // SSM_GROUPS
    for c in range(x_ref.shape[0] // ROW_CHUNK):
        rows = slice(c * ROW_CHUNK, (c + 1) * ROW_CHUNK)
        y = ((yf_ref[rows, :].astype(F32) + yb_ref[rows, :].astype(F32) + xs_ref[rows, :].astype(F32) * dsk_ref[...])
             * _silu(z_ref[rows, :].astype(F32)))
        ssm = jnp.concatenate([_rms_rows(y[:, g * gw:(g + 1) * gw], gn_ref[:, g * gw:(g + 1) * gw])
                               for g in range(SSM_GROUPS)], axis=1).astype(BF16)
        o_ref[rows, :] = (x_ref[rows, :]
                          + jnp.dot(att_ref[rows, :], wa_ref[...], preferred_element_type=F32)
                          + jnp.dot(ssm, ws_ref[...], preferred_element_type=F32))


def _mix_out(att, yf, yb, xbc, zx, dskip, gnorm, w_out, li, x, *, tm=512):
    m = x.shape[0]
    row = pl.BlockSpec((tm, SSM_WIDTH), lambda i: (i, 0))
    vec = pl.BlockSpec((1, SSM_WIDTH), lambda i: (0, 0))
    wsp = lambda half: pl.BlockSpec((None, SSM_WIDTH, D_MODEL), lambda i: (li, half, 0))
    full = pl.BlockSpec((tm, D_MODEL), lambda i: (i, 0))
    return pl.pallas_call(
        _mix_out_kernel,
        grid=(m // tm,),
        in_specs=[row, row, row, row, row, vec, vec, wsp(0), wsp(1), full],
        out_specs=full,
        out_shape=jax.ShapeDtypeStruct((m, D_MODEL), F32),
        compiler_params=_cparams(("parallel",)),
        name="mix_out",
    )(att, yf, yb, xbc, zx, dskip, gnorm, w_out, w_out, x)


def _pw1_glu_kernel(x_ref, g_ref, w_ref, b_ref, o_ref):
    for c in range(x_ref.shape[0] // ROW_CHUNK):
        rows = slice(c * ROW_CHUNK, (c + 1) * ROW_CHUNK)
        xn = _rms_rows(x_ref[rows, :], g_ref[...]).astype(BF16)
        a = jnp.dot(xn, w_ref[:, :D_MODEL], preferred_element_type=F32) + b_ref[:, :D_MODEL]
        gate = jnp.dot(xn, w_ref[:, D_MODEL:], preferred_element_type=F32) + b_ref[:, D_MODEL:]
        o_ref[rows, :] = (a * _sigmoid(gate)).astype(o_ref.dtype)


def _pw1_glu(x, g, w, li, b, *, tm=256):
    m = x.shape[0]
    full = pl.BlockSpec((tm, D_MODEL), lambda i: (i, 0))
    return pl.pallas_call(
        _pw1_glu_kernel,
        grid=(m // tm,),
        in_specs=[
            full,
            pl.BlockSpec((1, D_MODEL), lambda i: (0, 0)),
            pl.BlockSpec((None, D_MODEL, 2 * D_MODEL), lambda i: (li, 0, 0)),
            pl.BlockSpec((1, 2 * D_MODEL), lambda i: (0, 0)),
        ],
        out_specs=full,
        out_shape=jax.ShapeDtypeStruct((m, D_MODEL), F32),
        compiler_params=_cparams(("parallel",)),
        name="pw1_glu",
    )(x, g, w, b)


def _pw2_kernel(u_ref, lg_ref, lb_ref, w_ref, b_ref, x_ref, o_ref):
    for c in range(x_ref.shape[0] // ROW_CHUNK):
        rows = slice(c * ROW_CHUNK, (c + 1) * ROW_CHUNK)
        u = u_ref[rows, :]
        mu = jnp.mean(u, axis=-1, keepdims=True)
        uc = u - mu
        var = jnp.mean(uc * uc, axis=-1, keepdims=True)
        un = _silu(uc * lax.rsqrt(var + EPS) * lg_ref[...] + lb_ref[...]).astype(BF16)
        o_ref[rows, :] = x_ref[rows, :] + b_ref[...] + jnp.dot(un, w_ref[...], preferred_element_type=F32)


def _pw2(u, ln_g, ln_b, w, li, b, x, *, tm=512):
    m = x.shape[0]
    vec = pl.BlockSpec((1, D_MODEL), lambda i: (0, 0))
    full = pl.BlockSpec((tm, D_MODEL), lambda i: (i, 0))
    return pl.pallas_call(
        _pw2_kernel,
        grid=(m // tm,),
        in_specs=[full, vec, vec, pl.BlockSpec((None, D_MODEL, D_MODEL), lambda i: (li, 0, 0)), vec, full],
        out_specs=full,
        out_shape=jax.ShapeDtypeStruct((m, D_MODEL), F32),
        compiler_params=_cparams(("parallel",)),
        name="pw2",
    )(u, ln_g, ln_b, w, b, x)


def _ffn_kernel(x_ref, g_ref, wg_ref, wu_ref, wd_ref, o_ref, xn_ref):
    @pl.when(pl.program_id(1) == 0)
    def _():
        x = x_ref[...]
        xn_ref[...] = _rms_rows(x, g_ref[...]).astype(BF16)
        o_ref[...] = x

    xn = xn_ref[...]
    gate = jnp.dot(xn, wg_ref[...], preferred_element_type=F32)
    up = jnp.dot(xn, wu_ref[...], preferred_element_type=F32)
    h = (_silu(gate) * up).astype(BF16)
    o_ref[...] += jnp.dot(h, wd_ref[...], preferred_element_type=F32)


def _ffn(x, g, wg, wu, wd, li, *, tm=1024, th=256):
    m = x.shape[0]
    return pl.pallas_call(
        _ffn_kernel,
        grid=(m // tm, FFN_HIDDEN // th),
        in_specs=[
            pl.BlockSpec((tm, D_MODEL), lambda i, j: (i, 0)),
            pl.BlockSpec((1, D_MODEL), lambda i, j: (0, 0)),
            pl.BlockSpec((None, D_MODEL, th), lambda i, j: (li, 0, j)),
            pl.BlockSpec((None, D_MODEL, th), lambda i, j: (li, 0, j)),
            pl.BlockSpec((None, th, D_MODEL), lambda i, j: (li, j, 0)),
        ],
        out_specs=pl.BlockSpec((tm, D_MODEL), lambda i, j: (i, 0)),
        out_shape=jax.ShapeDtypeStruct((m, D_MODEL), F32),
        scratch_shapes=[pltpu.VMEM((tm, D_MODEL), BF16)],
        compiler_params=_cparams(("parallel", "arbitrary")),
        name="ffn",
    )(x, g, wg, wu, wd)


def _rope_tables(seq):
    pos = jnp.arange(seq, dtype=F32)
    inv_freq = ROPE_THETA ** (-jnp.arange(0, ROPE_DIM, 2, dtype=F32) / ROPE_DIM)
    ang = pos[:, None] * inv_freq[None, :]
    cos, sin = jnp.cos(ang), jnp.sin(ang)
    half = ROPE_DIM // 2
    zeros = lambda n: jnp.zeros((seq, n), F32)
    keep = jnp.concatenate([cos, cos, jnp.ones((seq, HEAD_DIM - ROPE_DIM), F32)], axis=-1)
    from_prev = jnp.concatenate([zeros(half), sin, zeros(HEAD_DIM - ROPE_DIM)], axis=-1)
    from_next = jnp.concatenate([-sin, zeros(HEAD_DIM - half)], axis=-1)
    return keep, from_prev, from_next


def _bf16_weights(p):
    w_in = p['w_in'].astype(BF16)
    return dict(
        w_in=w_in, w_zx=w_in[:, :, QK_COLS + ATT_WIDTH:PROJ_COLS],
        w_dt=jnp.pad(w_in[:, :, PROJ_COLS:], ((0, 0), (0, 0), (0, LANES - 2 * SSM_HEADS))),
        w_out=p['w_out'].astype(BF16), pw1_w=p['pw1_w'].astype(BF16), pw2_w=p['pw2_w'].astype(BF16),
        w_gate=p['w_gate'].astype(BF16), w_up=p['w_up'].astype(BF16), w_down=p['w_down'].astype(BF16))


def _trunk(x3, p, w):
    n_seq, seq, _ = x3.shape
    x = x3.reshape(n_seq * seq, D_MODEL)
    rope = _rope_tables(seq)
    row = lambda v: v.reshape(1, -1)
    for layer in range(DEPTH):
        if layer % 2 == 0:
            e = layer // 2
            head_gain = jnp.concatenate([jnp.tile(p['q_norm'][e] * (HEAD_DIM ** -0.5 * math.log2(math.e)), ATT_HEADS),
                                         jnp.tile(p['k_norm'][e], ATT_HEADS)]).reshape(1, QK_COLS)
            qk, v, zx, dt = _in_proj(x, row(p['mix_norm'][e]), w['w_in'], w['w_zx'], w['w_dt'], e,
                                     head_gain, rope, seq)
            att = _attention(qk, v, n_seq, seq)
            xbc = _dwconv(zx, p['ssm_conv_w'][e], p['ssm_conv_b'][e], n_seq, seq, SSM_WIDTH, SSM_CONV_CH, True, BF16, 512)
            dtt = jnp.transpose(dt[:, :2 * SSM_HEADS])
            yf, yb = _ssd(xbc, dt, dtt, p['dt_bias'][e], p['a_log'][e], n_seq, seq)
            dskip = jnp.repeat(p['d_skip'][e], SSM_HEAD_DIM).reshape(1, SSM_WIDTH)
            x = _mix_out(att, yf, yb, xbc, zx, dskip, row(p['ssm_norm'][e]), w['w_out'], e, x)
        else:
            o = layer // 2
            u = _pw1_glu(x, row(p['conf_norm'][o]), w['pw1_w'], o, row(p['pw1_b'][o]))
            u = _dwconv(u, p['dw_w'][o], p['dw_b'][o], n_seq, seq, 0, D_MODEL, False, F32, 256)
            x = _pw2(u, row(p['ln_g'][o]), row(p['ln_b'][o]), w['pw2_w'], o, row(p['pw2_b'][o]), x)
        x = _ffn(x, row(p['ffn_norm'][layer]), w['w_gate'], w['w_up'], w['w_down'], layer)
    return x.reshape(n_seq, seq, D_MODEL)


def kernel(x_prompt, x_sample, mix_norm, w_in, q_norm, k_norm, ssm_conv_w, ssm_conv_b, a_log, dt_bias, d_skip, ssm_norm, w_out, conf_norm, pw1_w, pw1_b, dw_w, dw_b, ln_g, ln_b, pw2_w, pw2_b, ffn_norm, w_gate, w_up, w_down):
    params = dict(mix_norm=mix_norm, w_in=w_in, q_norm=q_norm, k_norm=k_norm, ssm_conv_w=ssm_conv_w,
                  ssm_conv_b=ssm_conv_b, a_log=a_log, dt_bias=dt_bias, d_skip=d_skip, ssm_norm=ssm_norm,
                  w_out=w_out, conf_norm=conf_norm, pw1_w=pw1_w, pw1_b=pw1_b, dw_w=dw_w, dw_b=dw_b,
                  ln_g=ln_g, ln_b=ln_b, pw2_w=pw2_w, pw2_b=pw2_b, ffn_norm=ffn_norm, w_gate=w_gate,
                  w_up=w_up, w_down=w_down)
    weights = _bf16_weights(params)
    return (_trunk(x_prompt, params, weights), _trunk(x_sample, params, weights))
```

```python
import functools
import math

import jax
import jax.numpy as jnp
from jax import lax
from jax.experimental import pallas as pl
from jax.experimental.pallas import tpu as pltpu

F32 = jnp.float32
BF16 = jnp.bfloat16

D_MODEL = 2048
DEPTH = 4
HEAD_DIM = 128
ATT_HEADS = 8
ATT_WIDTH = ATT_HEADS * HEAD_DIM
ROPE_DIM = HEAD_DIM // 4
ROPE_THETA = 500000.0
ATT_HALF = 64
SSM_HEAD_DIM = 64
SSM_WIDTH = 1024
SSM_HEADS = 16
SSM_GROUPS = 2
SSM_STATE = 128
SSM_CONV = 4
SSM_CHUNK = 128
SSM_CONV_CH = SSM_WIDTH + 2 * SSM_GROUPS * SSM_STATE
PROJ_COLS = 3 * ATT_WIDTH + SSM_WIDTH + SSM_CONV_CH
CONF_KERNEL = 31
FFN_HIDDEN = 5632
EPS = 1e-6
NEG_INF = -1e30

LANES = 128
VMEM_LIMIT = 48 * 1024 * 1024
VMEM_LIMIT_FFN = 56 * 1024 * 1024


def _cparams(sem, limit=VMEM_LIMIT):
    return pltpu.CompilerParams(dimension_semantics=sem, vmem_limit_bytes=limit)


def _rms_rows(x, g):
    ms = jnp.mean(x * x, axis=-1, keepdims=True)
    return x * lax.rsqrt(ms + EPS) * g


def _silu(x):
    return x * (1.0 / (1.0 + jnp.exp(-x)))


def _sigmoid(x):
    return 1.0 / (1.0 + jnp.exp(-x))


QK_COLS = 2 * ATT_WIDTH
ZX_COLS = SSM_WIDTH + SSM_CONV_CH


def _in_proj_kernel(x_ref, g_ref, wqk_ref, wv_ref, wzx_ref, wdt_ref, hg_ref, ra_ref, rp_ref, rm_ref,
                    oqk_ref, ov_ref, ozx_ref, odt_ref, xn_ref):
    @pl.when(pl.program_id(1) == 0)
    def _():
        xn = _rms_rows(x_ref[...], g_ref[...]).astype(BF16)
        xn_ref[...] = xn
        odt_ref[...] = jnp.dot(xn, wdt_ref[...], preferred_element_type=F32)

    xn = xn_ref[...]
    acc = jnp.dot(xn, wqk_ref[...], preferred_element_type=F32)
    ov_ref[...] = jnp.dot(xn, wv_ref[...], preferred_element_type=F32)
    ozx_ref[...] = jnp.dot(xn, wzx_ref[...], preferred_element_type=F32).astype(BF16)
    ra, rp, rm = ra_ref[...], rp_ref[...], rm_ref[...]
    for hh in range(acc.shape[1] // HEAD_DIM):
        sl = slice(hh * HEAD_DIM, (hh + 1) * HEAD_DIM)
        t = _rms_rows(acc[:, sl], hg_ref[:, sl])
        oqk_ref[:, sl] = (t * ra + pltpu.roll(t, ROPE_DIM // 2, 1) * rp
                          + pltpu.roll(t, HEAD_DIM - ROPE_DIM // 2, 1) * rm)


def _in_proj(x, g, w_in, w_zx, wdt, li, head_gain, rope, seq, *, tm=512, steps=4):
    m = x.shape[0]
    tq, tv, tz = QK_COLS // steps, ATT_WIDTH // steps, ZX_COLS // steps
    pos_tiles = seq // tm
    rope_spec = pl.BlockSpec((tm, HEAD_DIM), lambda i, j: (i % pos_tiles, 0))
    wspec = lambda t, col0: pl.BlockSpec((None, D_MODEL, t), lambda i, j: (li, 0, col0 // t + j))
    ospec = lambda t: pl.BlockSpec((tm, t), lambda i, j: (i, j))
    return pl.pallas_call(
        _in_proj_kernel,
        grid=(m // tm, steps),
        in_specs=[
            pl.BlockSpec((tm, D_MODEL), lambda i, j: (i, 0)),
            pl.BlockSpec((1, D_MODEL), lambda i, j: (0, 0)),
            wspec(tq, 0), wspec(tv, QK_COLS), wspec(tz, 0),
            pl.BlockSpec((None, D_MODEL, LANES), lambda i, j: (li, 0, 0)),
            pl.BlockSpec((1, tq), lambda i, j: (0, j)),
            rope_spec, rope_spec, rope_spec,
        ],
        out_specs=[ospec(tq), ospec(tv), ospec(tz), pl.BlockSpec((tm, LANES), lambda i, j: (i, 0))],
        out_shape=[jax.ShapeDtypeStruct((m, QK_COLS), F32), jax.ShapeDtypeStruct((m, ATT_WIDTH), F32),
                   jax.ShapeDtypeStruct((m, ZX_COLS), BF16), jax.ShapeDtypeStruct((m, LANES), F32)],
        scratch_shapes=[pltpu.VMEM((tm, D_MODEL), BF16)],
        compiler_params=_cparams(("parallel", "arbitrary")),
        name="in_proj",
    )(x, g, w_in, w_in, w_zx, wdt, head_gain, *rope)


ATT_BQ = 128
ATT_KW = ATT_BQ + 2 * ATT_HALF
ATT_UNROLL = 16


def _attn_branch(q, k, v, state, prev_state, bias, cls_len, seq):
    acc, den, mx = state
    bq = min(ATT_BQ, cls_len)
    kw = min(ATT_KW, cls_len)
    nb = cls_len // bq
    n_prev = seq // (4 * cls_len)
    ones = jnp.ones((kw, HEAD_DIM), BF16)

    def load(t):
        r = t // nb
        b = t - r * nb
        kstart = jnp.clip(b * bq - ATT_HALF, 0, cls_len - kw)
        qoff = pl.multiple_of(r * cls_len + b * bq, ATT_HALF)
        koff = pl.multiple_of(r * cls_len + kstart, ATT_HALF)
        case = (b * bq - kstart) // ATT_HALF
        qv = q[pl.ds(qoff, bq), :].astype(BF16)
        kv = k[pl.ds(koff, kw), :].astype(BF16)
        vv = v[pl.ds(koff, kw), :].astype(BF16)
        bias_t = bias[case, :bq, :kw]
        old = None
        if prev_state is not None:
            a = r // n_prev
            c = r - a * n_prev
            prev_rows = pl.ds(c * (4 * cls_len) + a + 4 * b * bq, bq, stride=4)
            old = tuple(ref[prev_rows, :] for ref in (prev_state[2], prev_state[1], prev_state[0]))
        return qoff, qv, kv, vv, bias_t, old

    def compute(qv, kv, vv, bias_t, old):
        s = lax.dot_general(qv, kv, (((1,), (1,)), ((), ())), preferred_element_type=F32) + bias_t
        m_new = jnp.broadcast_to(jnp.max(s, axis=-1, keepdims=True), (bq, HEAD_DIM))
        if old is not None:
            m_new = jnp.maximum(old[0], m_new)
        p = jnp.exp2(s - jnp.concatenate([m_new] * (kw // HEAD_DIM), axis=1))
        pv = jnp.dot(p.astype(BF16), jnp.concatenate([vv, ones], axis=1), preferred_element_type=F32)
        num, rowsum = pv[:, :HEAD_DIM], pv[:, HEAD_DIM:]
        if old is not None:
            alpha = jnp.exp2(old[0] - m_new)
            num = alpha * old[2] + num
            rowsum = alpha * old[1] + rowsum
        return m_new, rowsum, num

    def body(t, carry):
        loaded = [load(t * ATT_UNROLL + u) for u in range(ATT_UNROLL)]
        results = [compute(*item[1:]) for item in loaded]
        for (qoff, *_), (m_new, d_new, a_new) in zip(loaded, results):
            mx[pl.ds(qoff, bq), :] = m_new
            den[pl.ds(qoff, bq), :] = d_new
            acc[pl.ds(qoff, bq), :] = a_new
        return carry

    lax.fori_loop(0, seq // bq // ATT_UNROLL, body, 0)


def _regroup_by_4(srcs, dsts, seq, src_cls):
    sub = src_cls // 4
    ch = min(sub, 128)
    n_per = sub // ch
    n_old = seq // src_cls

    def body(t, carry):
        c = t // (4 * n_per)
        rem = t - c * (4 * n_per)
        a = rem // n_per
        i = rem - a * n_per
        dst_off = pl.multiple_of((a * n_old + c) * sub + i * ch, ch)
        src_off = c * src_cls + a + 4 * i * ch
        for s_ref, d_refs in zip(srcs, dsts):
            val = s_ref[pl.ds(src_off, ch, stride=4), :]
            for d_ref in d_refs:
                d_ref[pl.ds(dst_off, ch), :] = val.astype(d_ref.dtype)
        return carry

    lax.fori_loop(0, n_old * 4 * n_per, body, 0, unroll=4)


def _attn_kernel(q_ref, k_ref, v_ref, o_ref, acc0, den0, mx0, q4, k4, v4, acc4, den4, mx4,
                 k4b, v4b, q16b, k16b, v16b, bias, *, seq):
    dist = (lax.broadcasted_iota(jnp.int32, (ATT_BQ, ATT_KW), 0)
            - lax.broadcasted_iota(jnp.int32, (ATT_BQ, ATT_KW), 1))
    for case in range(3):
        bias[case] = jnp.where(jnp.abs(dist + case * ATT_HALF) <= ATT_HALF, 0.0, NEG_INF)
    state0, state4 = (acc0, den0, mx0), (acc4, den4, mx4)
    _attn_branch(q_ref, k_ref, v_ref, state0, None, bias, seq, seq)
    _regroup_by_4((q_ref, k_ref, v_ref), ((q4,), (k4, k4b), (v4, v4b)), seq, seq)
    _attn_branch(q4, k4b, v4b, state4, state0, bias, seq // 4, seq)
    _regroup_by_4((q4, k4, v4), ((q16b,), (k16b,), (v16b,)), seq, seq // 4)
    _attn_branch(q16b, k16b, v16b, state0, state4, bias, seq // 16, seq)
    cls_len = seq // 16
    for r in range(16):
        rows = pl.ds(r * cls_len, cls_len)
        q4[pl.ds(r, cls_len, stride=16), :] = acc0[rows, :] * (1.0 / den0[rows, :])
    o_ref[...] = q4[...].astype(o_ref.dtype)


def _attention(qk, v, n_seq, seq):
    m = qk.shape[0]
    blk = lambda off: pl.BlockSpec((seq, HEAD_DIM), lambda b, h, off=off: (b, off + h))
    return pl.pallas_call(
        functools.partial(_attn_kernel, seq=seq),
        grid=(n_seq, ATT_HEADS),
        in_specs=[blk(0), blk(ATT_HEADS), blk(0)],
        out_specs=pl.BlockSpec((seq, HEAD_DIM), lambda b, h: (b, h)),
        out_shape=jax.ShapeDtypeStruct((m, ATT_WIDTH), BF16),
        scratch_shapes=([pltpu.VMEM((seq, HEAD_DIM), F32) for _ in range(9)]
                        + [pltpu.VMEM((seq, HEAD_DIM), BF16) for _ in range(5)]
                        + [pltpu.VMEM((3, ATT_BQ, ATT_KW), F32)]),
        compiler_params=_cparams(("parallel", "parallel")),
        name="dilated_attention",
    )(qk, qk, v)


CONV_PAD = 16
CONV_ROWS = 128


def _dwconv_kernel(x_ref, w_ref, b_ref, o_ref, pad_ref, *, seq, width, act):
    left = (width - 1) // 2
    zeros = jnp.zeros((CONV_PAD, LANES), F32)
    for slab in range(x_ref.shape[1] // LANES):
        lanes = slice(slab * LANES, (slab + 1) * LANES)
        pad = pad_ref.at[slab]
        pad[pl.ds(0, CONV_PAD), :] = zeros
        pad[pl.ds(CONV_PAD + seq, CONV_PAD), :] = zeros
        pad[pl.ds(CONV_PAD, seq), :] = x_ref[:, lanes].astype(F32)

        def body(i, carry, lanes=lanes, pad=pad):
            r0 = pl.multiple_of(i * CONV_ROWS, CONV_ROWS)
            acc = jnp.broadcast_to(b_ref[:, lanes], (CONV_ROWS, LANES))
            for tap in range(width):
                acc = acc + pad[pl.ds(r0 + CONV_PAD + tap - left, CONV_ROWS), :] * w_ref[tap:tap + 1, lanes]
            if act:
                acc = _silu(acc)
            o_ref[pl.ds(r0, CONV_ROWS), lanes] = acc.astype(o_ref.dtype)
            return carry

        lax.fori_loop(0, seq // CONV_ROWS, body, 0)


def _dwconv(x, w, b, n_seq, seq, col0, channels, act, out_dtype, cw):
    width = w.shape[0]
    cb0 = col0 // cw
    return pl.pallas_call(
        functools.partial(_dwconv_kernel, seq=seq, width=width, act=act),
        grid=(n_seq, channels // cw),
        in_specs=[
            pl.BlockSpec((seq, cw), lambda s, c: (s, cb0 + c)),
            pl.BlockSpec((width, cw), lambda s, c: (0, c)),
            pl.BlockSpec((1, cw), lambda s, c: (0, c)),
        ],
        out_specs=pl.BlockSpec((seq, cw), lambda s, c: (s, c)),
        out_shape=jax.ShapeDtypeStruct((n_seq * seq, channels), out_dtype),
        scratch_shapes=[pltpu.VMEM((cw // LANES, seq + 2 * CONV_PAD, LANES), F32)],
        compiler_params=_cparams(("parallel", "parallel")),
        name=f"dwconv{width}",
    )(x, w, b.reshape(1, channels))


def _split3(x):
    hi = x.astype(BF16)
    r1 = x - hi.astype(F32)
    mid = r1.astype(BF16)
    lo = (r1 - mid.astype(F32)).astype(BF16)
    return hi, mid, lo


def _dot_exact_rhs(a, b_bf16):
    return sum(jnp.dot(p, b_bf16, preferred_element_type=F32) for p in _split3(a))


def _dot_exact_lhs(a_bf16, b):
    return sum(jnp.dot(a_bf16, p, preferred_element_type=F32) for p in _split3(b))


def _softplus(x):
    return jnp.maximum(x, 0.0) + jnp.log(1.0 + jnp.exp(-jnp.abs(x)))


def _ssd_direction(xs_ref, b_ref, c_ref, dt_ref, dtt_ref, brow_ref, bcol_ref, arow_ref, acol_ref,
                   y_ref, h_ref, reverse):
    T = SSM_CHUNK
    d0 = SSM_HEADS if reverse else 0
    log2e = math.log2(math.e)
    adt_r = _softplus(dt_ref[...] + brow_ref[...]) * (-log2e * jnp.exp(arow_ref[...]))
    delta_c = _softplus(dtt_ref[d0:d0 + SSM_HEADS, :] + bcol_ref[d0:d0 + SSM_HEADS, :])
    adt_c = delta_c * (-log2e * jnp.exp(acol_ref[d0:d0 + SSM_HEADS, :]))

    ri = lax.broadcasted_iota(jnp.int32, (T, T), 0)
    ci = lax.broadcasted_iota(jnp.int32, (T, T), 1)
    causal = (ci >= ri) if reverse else (ci <= ri)
    tri = jnp.where(causal, 1.0, 0.0).astype(BF16)
    tri_t = jnp.where((ri >= ci) if reverse else (ri <= ci), 1.0, 0.0).astype(BF16)
    cum_r = _dot_exact_lhs(tri, adt_r)
    cum_c = _dot_exact_rhs(adt_c, tri_t)
    end = 0 if reverse else T - 1
    total_c = cum_c[:, end:end + 1]
    src_w_c = delta_c * jnp.exp2(total_c - cum_c)
    chunk_decay_c = jnp.exp2(total_c)

    lane_lt64 = lax.broadcasted_iota(jnp.int32, (T, LANES), 1) < SSM_HEAD_DIM
    pairs_per_group = SSM_HEADS // SSM_GROUPS // 2

    for g in range(SSM_GROUPS):
        bm = b_ref[:, g * SSM_STATE:(g + 1) * SSM_STATE]
        cm = c_ref[:, g * SSM_STATE:(g + 1) * SSM_STATE]
        cb = lax.dot_general(cm, bm, (((1,), (1,)), ((), ())), preferred_element_type=F32)
        bt = jnp.transpose(bm.astype(F32))
        cm32 = cm.astype(F32)
        for kk in range(pairs_per_group):
            k = g * pairs_per_group + kk
            lanes = slice(k * LANES, (k + 1) * LANES)
            cols = slice(kk * LANES, (kk + 1) * LANES)
            xs = xs_ref[:, lanes]
            h_in = h_ref[g, :, cols]
            rhs = jnp.concatenate([xs, h_in.astype(BF16)], axis=0)
            ys, sts = [], []
            for hh in (2 * k, 2 * k + 1):
                a_l = jnp.broadcast_to(cum_r[:, d0 + hh:d0 + hh + 1], (T, T))
                decay = jnp.exp2(jnp.where(causal, a_l - cum_c[hh:hh + 1, :], NEG_INF))
                gmat = cb * decay * delta_c[hh:hh + 1, :]
                lhs = jnp.concatenate([gmat, cm32 * jnp.exp2(a_l)], axis=1).astype(BF16)
                ys.append(jnp.dot(lhs, rhs, preferred_element_type=F32))
                bts = (bt * src_w_c[hh:hh + 1, :]).astype(BF16)
                sts.append(jnp.dot(bts, xs, preferred_element_type=F32))
            y_ref[:, lanes] = jnp.where(lane_lt64, ys[0], ys[1]).astype(y_ref.dtype)
            cd = jnp.where(lane_lt64, chunk_decay_c[2 * k:2 * k + 1, :], chunk_decay_c[2 * k + 1:2 * k + 2, :])
            h_ref[g, :, cols] = h_in * cd + jnp.where(lane_lt64, sts[0], sts[1])


def _ssd_kernel(xsf, bf, cf, dtf, dttf, xsb, bb, cb, dtb, dttb, brow, bcol, arow, acol, yf, yb, hf, hb):
    @pl.when(pl.program_id(1) == 0)
    def _():
        hf[...] = jnp.zeros_like(hf)
        hb[...] = jnp.zeros_like(hb)

    _ssd_direction(xsf, bf, cf, dtf, dttf, brow, bcol, arow, acol, yf, hf, False)
    _ssd_direction(xsb, bb, cb, dtb, dttb, brow, bcol, arow, acol, yb, hb, True)


def _ssd(xbc, dt, dtt, dt_bias, a_log, n_seq, seq):
    m = xbc.shape[0]
    T = SSM_CHUNK
    nc = seq // T
    fwd = lambda s, i: s * nc + i
    bwd = lambda s, i: s * nc + (nc - 1 - i)
    gn = SSM_GROUPS * SSM_STATE

    def specs(row):
        return [
            pl.BlockSpec((T, SSM_WIDTH), lambda s, i: (row(s, i), 0)),
            pl.BlockSpec((T, gn), lambda s, i: (row(s, i), SSM_WIDTH // gn)),
            pl.BlockSpec((T, gn), lambda s, i: (row(s, i), SSM_WIDTH // gn + 1)),
            pl.BlockSpec((T, LANES), lambda s, i: (row(s, i), 0)),
            pl.BlockSpec((2 * SSM_HEADS, T), lambda s, i: (0, row(s, i))),
        ]

    small = lambda shape: pl.BlockSpec(shape, lambda s, i: (0, 0))
    nh2 = 2 * SSM_HEADS
    lane_row = lambda v: jnp.pad(v.reshape(1, nh2), ((0, 0), (0, LANES - nh2)))
    return pl.pallas_call(
        _ssd_kernel,
        grid=(n_seq, nc),
        in_specs=specs(fwd) + specs(bwd) + [small((1, LANES)), small((nh2, 1)), small((1, LANES)), small((nh2, 1))],
        out_specs=[
            pl.BlockSpec((T, SSM_WIDTH), lambda s, i: (fwd(s, i), 0)),
            pl.BlockSpec((T, SSM_WIDTH), lambda s, i: (bwd(s, i), 0)),
        ],
        out_shape=[jax.ShapeDtypeStruct((m, SSM_WIDTH), BF16)] * 2,
        scratch_shapes=[pltpu.VMEM((SSM_GROUPS, SSM_STATE, SSM_WIDTH // SSM_GROUPS), F32)] * 2,
        compiler_params=_cparams(("parallel", "arbitrary")),
        name="ssd",
    )(xbc, xbc, xbc, dt, dtt, xbc, xbc, xbc, dt, dtt,
      lane_row(dt_bias), dt_bias.reshape(nh2, 1), lane_row(a_log), a_log.reshape(nh2, 1))


ROW_CHUNK = 128


def _mix_out_kernel(att_ref, yf_ref, yb_ref, xs_ref, z_ref, dsk_ref, gn_ref, wa_ref, ws_ref, x_ref, o_ref):
    gw = SSM_WIDTH // SSM_GROUPS
    for c in range(x_ref.shape[0] // ROW_CHUNK):
        rows = slice(c * ROW_CHUNK, (c + 1) * ROW_CHUNK)
        y = ((yf_ref[rows, :].astype(F32) + yb_ref[rows, :].astype(F32) + xs_ref[rows, :].astype(F32) * dsk_ref[...])
             * _silu(z_ref[rows, :].astype(F32)))
        ssm = jnp.concatenate([_rms_rows(y[:, g * gw:(g + 1) * gw], gn_ref[:, g * gw:(g + 1) * gw])
                               for g in range(SSM_GROUPS)], axis=1).astype(BF16)
        o_ref[rows, :] = (x_ref[rows, :]
                          + jnp.dot(att_ref[rows, :], wa_ref[...], preferred_element_type=F32)
                          + jnp.dot(ssm, ws_ref[...], preferred_element_type=F32))


def _mix_out(att, yf, yb, xbc, zx, dskip, gnorm, w_out, li, x, *, tm=512):
    m = x.shape[0]
    row = pl.BlockSpec((tm, SSM_WIDTH), lambda i: (i, 0))
    vec = pl.BlockSpec((1, SSM_WIDTH), lambda i: (0, 0))
    wsp = lambda half: pl.BlockSpec((None, SSM_WIDTH, D_MODEL), lambda i: (li, half, 0))
    full = pl.BlockSpec((tm, D_MODEL), lambda i: (i, 0))
    return pl.pallas_call(
        _mix_out_kernel,
        grid=(m // tm,),
        in_specs=[row, row, row, row, row, vec, vec, wsp(0), wsp(1), full],
        out_specs=full,
        out_shape=jax.ShapeDtypeStruct((m, D_MODEL), F32),
        compiler_params=_cparams(("parallel",)),
        name="mix_out",
    )(att, yf, yb, xbc, zx, dskip, gnorm, w_out, w_out, x)


def _pw1_glu_kernel(x_ref, g_ref, w_ref, b_ref, o_ref):
    for c in range(x_ref.shape[0] // ROW_CHUNK):
        rows = slice(c * ROW_CHUNK, (c + 1) * ROW_CHUNK)
        xn = _rms_rows(x_ref[rows, :], g_ref[...]).astype(BF16)
        a = jnp.dot(xn, w_ref[:, :D_MODEL], preferred_element_type=F32) + b_ref[:, :D_MODEL]
        gate = jnp.dot(xn, w_ref[:, D_MODEL:], preferred_element_type=F32) + b_ref[:, D_MODEL:]
        o_ref[rows, :] = (a * _sigmoid(gate)).astype(o_ref.dtype)


def _pw1_glu(x, g, w, li, b, *, tm=256):
    m = x.shape[0]
    full = pl.BlockSpec((tm, D_MODEL), lambda i: (i, 0))
    return pl.pallas_call(
        _pw1_glu_kernel,
        grid=(m // tm,),
        in_specs=[
            full,
            pl.BlockSpec((1, D_MODEL), lambda i: (0, 0)),
            pl.BlockSpec((None, D_MODEL, 2 * D_MODEL), lambda i: (li, 0, 0)),
            pl.BlockSpec((1, 2 * D_MODEL), lambda i: (0, 0)),
        ],
        out_specs=full,
        out_shape=jax.ShapeDtypeStruct((m, D_MODEL), F32),
        compiler_params=_cparams(("parallel",)),
        name="pw1_glu",
    )(x, g, w, b)


def _pw2_kernel(u_ref, lg_ref, lb_ref, w_ref, b_ref, x_ref, o_ref):
    for c in range(x_ref.shape[0] // ROW_CHUNK):
        rows = slice(c * ROW_CHUNK, (c + 1) * ROW_CHUNK)
        u = u_ref[rows, :]
        mu = jnp.mean(u, axis=-1, keepdims=True)
        uc = u - mu
        var = jnp.mean(uc * uc, axis=-1, keepdims=True)
        un = _silu(uc * lax.rsqrt(var + EPS) * lg_ref[...] + lb_ref[...]).astype(BF16)
        o_ref[rows, :] = x_ref[rows, :] + b_ref[...] + jnp.dot(un, w_ref[...], preferred_element_type=F32)


def _pw2(u, ln_g, ln_b, w, li, b, x, *, tm=512):
    m = x.shape[0]
    vec = pl.BlockSpec((1, D_MODEL), lambda i: (0, 0))
    full = pl.BlockSpec((tm, D_MODEL), lambda i: (i, 0))
    return pl.pallas_call(
        _pw2_kernel,
        grid=(m // tm,),
        in_specs=[full, vec, vec, pl.BlockSpec((None, D_MODEL, D_MODEL), lambda i: (li, 0, 0)), vec, full],
        out_specs=full,
        out_shape=jax.ShapeDtypeStruct((m, D_MODEL), F32),
        compiler_params=_cparams(("parallel",)),
        name="pw2",
    )(u, ln_g, ln_b, w, b, x)


def _ffn_kernel(x_ref, g_ref, wg_ref, wu_ref, wd_ref, o_ref, xn_ref):
    @pl.when(pl.program_id(1) == 0)
    def _():
        x = x_ref[...]
        xn_ref[...] = _rms_rows(x, g_ref[...]).astype(BF16)
        o_ref[...] = x

    xn = xn_ref[...]
    gate = jnp.dot(xn, wg_ref[...], preferred_element_type=F32)
    up = jnp.dot(xn, wu_ref[...], preferred_element_type=F32)
    h = (_silu(gate) * up).astype(BF16)
    o_ref[...] += jnp.dot(h, wd_ref[...], preferred_element_type=F32)


def _ffn(x, g, wg, wu, wd, li, *, tm=1024, th=512):
    m = x.shape[0]
    return pl.pallas_call(
        _ffn_kernel,
        grid=(m // tm, FFN_HIDDEN // th),
        in_specs=[
            pl.BlockSpec((tm, D_MODEL), lambda i, j: (i, 0)),
            pl.BlockSpec((1, D_MODEL), lambda i, j: (0, 0)),
            pl.BlockSpec((None, D_MODEL, th), lambda i, j: (li, 0, j)),
            pl.BlockSpec((None, D_MODEL, th), lambda i, j: (li, 0, j)),
            pl.BlockSpec((None, th, D_MODEL), lambda i, j: (li, j, 0)),
        ],
        out_specs=pl.BlockSpec((tm, D_MODEL), lambda i, j: (i, 0)),
        out_shape=jax.ShapeDtypeStruct((m, D_MODEL), F32),
        scratch_shapes=[pltpu.VMEM((tm, D_MODEL), BF16)],
        compiler_params=_cparams(("parallel", "arbitrary"), VMEM_LIMIT_FFN),
        name="ffn",
    )(x, g, wg, wu, wd)


def _rope_tables(seq):
    pos = jnp.arange(seq, dtype=F32)
    inv_freq = ROPE_THETA ** (-jnp.arange(0, ROPE_DIM, 2, dtype=F32) / ROPE_DIM)
    ang = pos[:, None] * inv_freq[None, :]
    cos, sin = jnp.cos(ang), jnp.sin(ang)
    half = ROPE_DIM // 2
    zeros = lambda n: jnp.zeros((seq, n), F32)
    keep = jnp.concatenate([cos, cos, jnp.ones((seq, HEAD_DIM - ROPE_DIM), F32)], axis=-1)
    from_prev = jnp.concatenate([zeros(half), sin, zeros(HEAD_DIM - ROPE_DIM)], axis=-1)
    from_next = jnp.concatenate([-sin, zeros(HEAD_DIM - half)], axis=-1)
    return keep, from_prev, from_next


def _bf16_weights(p):
    w_in = p['w_in'].astype(BF16)
    return dict(
        w_in=w_in, w_zx=w_in[:, :, QK_COLS + ATT_WIDTH:PROJ_COLS],
        w_dt=jnp.pad(w_in[:, :, PROJ_COLS:], ((0, 0), (0, 0), (0, LANES - 2 * SSM_HEADS))),
        w_out=p['w_out'].astype(BF16), pw1_w=p['pw1_w'].astype(BF16), pw2_w=p['pw2_w'].astype(BF16),
        w_gate=p['w_gate'].astype(BF16), w_up=p['w_up'].astype(BF16), w_down=p['w_down'].astype(BF16))


def _trunk(x3, p, w):
    n_seq, seq, _ = x3.shape
    x = x3.reshape(n_seq * seq, D_MODEL)
    rope = _rope_tables(seq)
    row = lambda v: v.reshape(1, -1)
    for layer in range(DEPTH):
        if layer % 2 == 0:
            e = layer // 2
            head_gain = jnp.concatenate([jnp.tile(p['q_norm'][e] * (HEAD_DIM ** -0.5 * math.log2(math.e)), ATT_HEADS),
                                         jnp.tile(p['k_norm'][e], ATT_HEADS)]).reshape(1, QK_COLS)
            qk, v, zx, dt = _in_proj(x, row(p['mix_norm'][e]), w['w_in'], w['w_zx'], w['w_dt'], e,
                                     head_gain, rope, seq)
            att = _attention(qk, v, n_seq, seq)
            xbc = _dwconv(zx, p['ssm_conv_w'][e], p['ssm_conv_b'][e], n_seq, seq, SSM_WIDTH, SSM_CONV_CH, True, BF16, 512)
            dtt = jnp.transpose(dt[:, :2 * SSM_HEADS])
            yf, yb = _ssd(xbc, dt, dtt, p['dt_bias'][e], p['a_log'][e], n_seq, seq)
            dskip = jnp.repeat(p['d_skip'][e], SSM_HEAD_DIM).reshape(1, SSM_WIDTH)
            x = _mix_out(att, yf, yb, xbc, zx, dskip, row(p['ssm_norm'][e]), w['w_out'], e, x)
        else:
            o = layer // 2
            u = _pw1_glu(x, row(p['conf_norm'][o]), w['pw1_w'], o, row(p['pw1_b'][o]))
            u = _dwconv(u, p['dw_w'][o], p['dw_b'][o], n_seq, seq, 0, D_MODEL, False, F32, 256)
            x = _pw2(u, row(p['ln_g'][o]), row(p['ln_b'][o]), w['pw2_w'], o, row(p['pw2_b'][o]), x)
        x = _ffn(x, row(p['ffn_norm'][layer]), w['w_gate'], w['w_up'], w['w_down'], layer)
    return x.reshape(n_seq, seq, D_MODEL)


def kernel(x_prompt, x_sample, mix_norm, w_in, q_norm, k_norm, ssm_conv_w, ssm_conv_b, a_log, dt_bias, d_skip, ssm_norm, w_out, conf_norm, pw1_w, pw1_b, dw_w, dw_b, ln_g, ln_b, pw2_w, pw2_b, ffn_norm, w_gate, w_up, w_down):
    params = dict(mix_norm=mix_norm, w_in=w_in, q_norm=q_norm, k_norm=k_norm, ssm_conv_w=ssm_conv_w,
                  ssm_conv_b=ssm_conv_b, a_log=a_log, dt_bias=dt_bias, d_skip=d_skip, ssm_norm=ssm_norm,
                  w_out=w_out, conf_norm=conf_norm, pw1_w=pw1_w, pw1_b=pw1_b, dw_w=dw_w, dw_b=dw_b,
                  ln_g=ln_g, ln_b=ln_b, pw2_w=pw2_w, pw2_b=pw2_b, ffn_norm=ffn_norm, w_gate=w_gate,
                  w_up=w_up, w_down=w_down)
    weights = _bf16_weights(params)
    return (_trunk(x_prompt, params, weights), _trunk(x_sample, params, weights))
```

```python
import functools
import math

import jax
import jax.numpy as jnp
from jax import lax
from jax.experimental import pallas as pl
from jax.experimental.pallas import tpu as pltpu

F32 = jnp.float32
BF16 = jnp.bfloat16

D_MODEL = 2048
DEPTH = 4
HEAD_DIM = 128
ATT_HEADS = 8
ATT_WIDTH = ATT_HEADS * HEAD_DIM
ROPE_DIM = HEAD_DIM // 4
ROPE_THETA = 500000.0
ATT_HALF = 64
SSM_HEAD_DIM = 64
SSM_WIDTH = 1024
SSM_HEADS = 16
SSM_GROUPS = 2
SSM_STATE = 128
SSM_CONV = 4
SSM_CHUNK = 128
SSM_CONV_CH = SSM_WIDTH + 2 * SSM_GROUPS * SSM_STATE
PROJ_COLS = 3 * ATT_WIDTH + SSM_WIDTH + SSM_CONV_CH
CONF_KERNEL = 31
FFN_HIDDEN = 5632
EPS = 1e-6
NEG_INF = -1e30

LANES = 128
VMEM_LIMIT = 48 * 1024 * 1024
VMEM_LIMIT_FFN = 56 * 1024 * 1024


def _cparams(sem, limit=VMEM_LIMIT):
    return pltpu.CompilerParams(dimension_semantics=sem, vmem_limit_bytes=limit)


def _rms_rows(x, g):
    ms = jnp.mean(x * x, axis=-1, keepdims=True)
    return x * lax.rsqrt(ms + EPS) * g


def _silu(x):
    return x * (1.0 / (1.0 + jnp.exp(-x)))


def _sigmoid(x):
    return 1.0 / (1.0 + jnp.exp(-x))


QK_COLS = 2 * ATT_WIDTH
ZX_COLS = SSM_WIDTH + SSM_CONV_CH


def _in_proj_kernel(x_ref, g_ref, wqk_ref, wv_ref, wzx_ref, wdt_ref, hg_ref, ra_ref, rp_ref, rm_ref,
                    oqk_ref, ov_ref, ozx_ref, odt_ref, xn_ref):
    @pl.when(pl.program_id(1) == 0)
    def _():
        xn = _rms_rows(x_ref[...], g_ref[...]).astype(BF16)
        xn_ref[...] = xn
        odt_ref[...] = jnp.dot(xn, wdt_ref[...], preferred_element_type=F32)

    xn = xn_ref[...]
    acc = jnp.dot(xn, wqk_ref[...], preferred_element_type=F32)
    ov_ref[...] = jnp.dot(xn, wv_ref[...], preferred_element_type=F32)
    ozx_ref[...] = jnp.dot(xn, wzx_ref[...], preferred_element_type=F32).astype(BF16)
    ra, rp, rm = ra_ref[...], rp_ref[...], rm_ref[...]
    for hh in range(acc.shape[1] // HEAD_DIM):
        sl = slice(hh * HEAD_DIM, (hh + 1) * HEAD_DIM)
        t = _rms_rows(acc[:, sl], hg_ref[:, sl])
        oqk_ref[:, sl] = (t * ra + pltpu.roll(t, ROPE_DIM // 2, 1) * rp
                          + pltpu.roll(t, HEAD_DIM - ROPE_DIM // 2, 1) * rm)


def _in_proj(x, g, w_in, w_zx, wdt, li, head_gain, rope, seq, *, tm=512, steps=4):
    m = x.shape[0]
    tq, tv, tz = QK_COLS // steps, ATT_WIDTH // steps, ZX_COLS // steps
    pos_tiles = seq // tm
    rope_spec = pl.BlockSpec((tm, HEAD_DIM), lambda i, j: (i % pos_tiles, 0))
    wspec = lambda t, col0: pl.BlockSpec((None, D_MODEL, t), lambda i, j: (li, 0, col0 // t + j))
    ospec = lambda t: pl.BlockSpec((tm, t), lambda i, j: (i, j))
    return pl.pallas_call(
        _in_proj_kernel,
        grid=(m // tm, steps),
        in_specs=[
            pl.BlockSpec((tm, D_MODEL), lambda i, j: (i, 0)),
            pl.BlockSpec((1, D_MODEL), lambda i, j: (0, 0)),
            wspec(tq, 0), wspec(tv, QK_COLS), wspec(tz, 0),
            pl.BlockSpec((None, D_MODEL, LANES), lambda i, j: (li, 0, 0)),
            pl.BlockSpec((1, tq), lambda i, j: (0, j)),
            rope_spec, rope_spec, rope_spec,
        ],
        out_specs=[ospec(tq), ospec(tv), ospec(tz), pl.BlockSpec((tm, LANES), lambda i, j: (i, 0))],
        out_shape=[jax.ShapeDtypeStruct((m, QK_COLS), F32), jax.ShapeDtypeStruct((m, ATT_WIDTH), F32),
                   jax.ShapeDtypeStruct((m, ZX_COLS), BF16), jax.ShapeDtypeStruct((m, LANES), F32)],
        scratch_shapes=[pltpu.VMEM((tm, D_MODEL), BF16)],
        compiler_params=_cparams(("parallel", "arbitrary")),
        name="in_proj",
    )(x, g, w_in, w_in, w_zx, wdt, head_gain, *rope)


ATT_BQ = 128
ATT_KW = ATT_BQ + 2 * ATT_HALF
ATT_UNROLL = 16


def _attn_branch(q, k, v, state, prev_state, bias, cls_len, seq):
    acc, den, mx = state
    bq = min(ATT_BQ, cls_len)
    kw = min(ATT_KW, cls_len)
    nb = cls_len // bq
    n_prev = seq // (4 * cls_len)
    ones = jnp.ones((kw, HEAD_DIM), BF16)

    def load(t):
        r = t // nb
        b = t - r * nb
        kstart = jnp.clip(b * bq - ATT_HALF, 0, cls_len - kw)
        qoff = pl.multiple_of(r * cls_len + b * bq, ATT_HALF)
        koff = pl.multiple_of(r * cls_len + kstart, ATT_HALF)
        case = (b * bq - kstart) // ATT_HALF
        qv = q[pl.ds(qoff, bq), :].astype(BF16)
        kv = k[pl.ds(koff, kw), :].astype(BF16)
        vv = v[pl.ds(koff, kw), :].astype(BF16)
        bias_t = bias[case, :bq, :kw]
        old = None
        if prev_state is not None:
            a = r // n_prev
            c = r - a * n_prev
            prev_rows = pl.ds(c * (4 * cls_len) + a + 4 * b * bq, bq, stride=4)
            old = tuple(ref[prev_rows, :] for ref in (prev_state[2], prev_state[1], prev_state[0]))
        return qoff, qv, kv, vv, bias_t, old

    def compute(qv, kv, vv, bias_t, old):
        s = lax.dot_general(qv, kv, (((1,), (1,)), ((), ())), preferred_element_type=F32) + bias_t
        m_new = jnp.broadcast_to(jnp.max(s, axis=-1, keepdims=True), (bq, HEAD_DIM))
        if old is not None:
            m_new = jnp.maximum(old[0], m_new)
        p = jnp.exp2(s - jnp.concatenate([m_new] * (kw // HEAD_DIM), axis=1))
        pv = jnp.dot(p.astype(BF16), jnp.concatenate([vv, ones], axis=1), preferred_element_type=F32)
        num, rowsum = pv[:, :HEAD_DIM], pv[:, HEAD_DIM:]
        if old is not None:
            alpha = jnp.exp2(old[0] - m_new)
            num = alpha * old[2] + num
            rowsum = alpha * old[1] + rowsum
        return m_new, rowsum, num

    def body(t, carry):
        loaded = [load(t * ATT_UNROLL + u) for u in range(ATT_UNROLL)]
        results = [compute(*item[1:]) for item in loaded]
        for (qoff, *_), (m_new, d_new, a_new) in zip(loaded, results):
            mx[pl.ds(qoff, bq), :] = m_new
            den[pl.ds(qoff, bq), :] = d_new
            acc[pl.ds(qoff, bq), :] = a_new
        return carry

    lax.fori_loop(0, seq // bq // ATT_UNROLL, body, 0)


def _regroup_by_4(srcs, dsts, seq, src_cls):
    sub = src_cls // 4
    ch = min(sub, 128)
    n_per = sub // ch
    n_old = seq // src_cls

    def body(t, carry):
        c = t // (4 * n_per)
        rem = t - c * (4 * n_per)
        a = rem // n_per
        i = rem - a * n_per
        dst_off = pl.multiple_of((a * n_old + c) * sub + i * ch, ch)
        src_off = c * src_cls + a + 4 * i * ch
        for s_ref, d_refs in zip(srcs, dsts):
            val = s_ref[pl.ds(src_off, ch, stride=4), :]
            for d_ref in d_refs:
                d_ref[pl.ds(dst_off, ch), :] = val.astype(d_ref.dtype)
        return carry

    lax.fori_loop(0, n_old * 4 * n_per, body, 0, unroll=4)


def _attn_kernel(q_ref, k_ref, v_ref, o_ref, acc0, den0, mx0, q4, k4, v4, acc4, den4, mx4,
                 k4b, v4b, q16b, k16b, v16b, bias, *, seq):
    dist = (lax.broadcasted_iota(jnp.int32, (ATT_BQ, ATT_KW), 0)
            - lax.broadcasted_iota(jnp.int32, (ATT_BQ, ATT_KW), 1))
    for case in range(3):
        bias[case] = jnp.where(jnp.abs(dist + case * ATT_HALF) <= ATT_HALF, 0.0, NEG_INF)
    state0, state4 = (acc0, den0, mx0), (acc4, den4, mx4)
    _attn_branch(q_ref, k_ref, v_ref, state0, None, bias, seq, seq)
    _regroup_by_4((q_ref, k_ref, v_ref), ((q4,), (k4, k4b), (v4, v4b)), seq, seq)
    _attn_branch(q4, k4b, v4b, state4, state0, bias, seq // 4, seq)
    _regroup_by_4((q4, k4, v4), ((q16b,), (k16b,), (v16b,)), seq, seq // 4)
    _attn_branch(q16b, k16b, v16b, state0, state4, bias, seq // 16, seq)
    cls_len = seq // 16
    for r in range(16):
        rows = pl.ds(r * cls_len, cls_len)
        q4[pl.ds(r, cls_len, stride=16), :] = acc0[rows, :] * (1.0 / den0[rows, :])
    o_ref[...] = q4[...].astype(o_ref.dtype)


def _attention(qk, v, n_seq, seq):
    m = qk.shape[0]
    blk = lambda off: pl.BlockSpec((seq, HEAD_DIM), lambda b, h, off=off: (b, off + h))
    return pl.pallas_call(
        functools.partial(_attn_kernel, seq=seq),
        grid=(n_seq, ATT_HEADS),
        in_specs=[blk(0), blk(ATT_HEADS), blk(0)],
        out_specs=pl.BlockSpec((seq, HEAD_DIM), lambda b, h: (b, h)),
        out_shape=jax.ShapeDtypeStruct((m, ATT_WIDTH), BF16),
        scratch_shapes=([pltpu.VMEM((seq, HEAD_DIM), F32) for _ in range(9)]
                        + [pltpu.VMEM((seq, HEAD_DIM), BF16) for _ in range(5)]
                        + [pltpu.VMEM((3, ATT_BQ, ATT_KW), F32)]),
        compiler_params=_cparams(("parallel", "parallel")),
        name="dilated_attention",
    )(qk, qk, v)


CONV_PAD = 16
CONV_ROWS = 128


def _dwconv_kernel(x_ref, w_ref, b_ref, o_ref, pad_ref, *, seq, width, act):
    left = (width - 1) // 2
    zeros = jnp.zeros((CONV_PAD, LANES), F32)
    for slab in range(x_ref.shape[1] // LANES):
        lanes = slice(slab * LANES, (slab + 1) * LANES)
        pad = pad_ref.at[slab]
        pad[pl.ds(0, CONV_PAD), :] = zeros
        pad[pl.ds(CONV_PAD + seq, CONV_PAD), :] = zeros
        pad[pl.ds(CONV_PAD, seq), :] = x_ref[:, lanes].astype(F32)

        def body(i, carry, lanes=lanes, pad=pad):
            r0 = pl.multiple_of(i * CONV_ROWS, CONV_ROWS)
            acc = jnp.broadcast_to(b_ref[:, lanes], (CONV_ROWS, LANES))
            for tap in range(width):
                acc = acc + pad[pl.ds(r0 + CONV_PAD + tap - left, CONV_ROWS), :] * w_ref[tap:tap + 1, lanes]
            if act:
                acc = _silu(acc)
            o_ref[pl.ds(r0, CONV_ROWS), lanes] = acc.astype(o_ref.dtype)
            return carry

        lax.fori_loop(0, seq // CONV_ROWS, body, 0)


def _dwconv(x, w, b, n_seq, seq, col0, channels, act, out_dtype, cw):
    width = w.shape[0]
    cb0 = col0 // cw
    return pl.pallas_call(
        functools.partial(_dwconv_kernel, seq=seq, width=width, act=act),
        grid=(n_seq, channels // cw),
        in_specs=[
            pl.BlockSpec((seq, cw), lambda s, c: (s, cb0 + c)),
            pl.BlockSpec((width, cw), lambda s, c: (0, c)),
            pl.BlockSpec((1, cw), lambda s, c: (0, c)),
        ],
        out_specs=pl.BlockSpec((seq, cw), lambda s, c: (s, c)),
        out_shape=jax.ShapeDtypeStruct((n_seq * seq, channels), out_dtype),
        scratch_shapes=[pltpu.VMEM((cw // LANES, seq + 2 * CONV_PAD, LANES), F32)],
        compiler_params=_cparams(("parallel", "parallel")),
        name=f"dwconv{width}",
    )(x, w, b.reshape(1, channels))


def _split3(x):
    hi = x.astype(BF16)
    r1 = x - hi.astype(F32)
    mid = r1.astype(BF16)
    lo = (r1 - mid.astype(F32)).astype(BF16)
    return hi, mid, lo


def _dot_exact_rhs(a, b_bf16):
    return sum(jnp.dot(p, b_bf16, preferred_element_type=F32) for p in _split3(a))


def _dot_exact_lhs(a_bf16, b):
    return sum(jnp.dot(a_bf16, p, preferred_element_type=F32) for p in _split3(b))


def _softplus(x):
    return jnp.maximum(x, 0.0) + jnp.log(1.0 + jnp.exp(-jnp.abs(x)))


def _ssd_direction(xs_ref, b_ref, c_ref, dt_ref, dtt_ref, brow_ref, bcol_ref, arow_ref, acol_ref,
                   y_ref, h_ref, reverse):
    T = SSM_CHUNK
    d0 = SSM_HEADS if reverse else 0
    log2e = math.log2(math.e)
    adt_r = _softplus(dt_ref[...] + brow_ref[...]) * (-log2e * jnp.exp(arow_ref[...]))
    delta_c = _softplus(dtt_ref[d0:d0 + SSM_HEADS, :] + bcol_ref[d0:d0 + SSM_HEADS, :])
    adt_c = delta_c * (-log2e * jnp.exp(acol_ref[d0:d0 + SSM_HEADS, :]))

    ri = lax.broadcasted_iota(jnp.int32, (T, T), 0)
    ci = lax.broadcasted_iota(jnp.int32, (T, T), 1)
    causal = (ci >= ri) if reverse else (ci <= ri)
    tri = jnp.where(causal, 1.0, 0.0).astype(BF16)
    tri_t = jnp.where((ri >= ci) if reverse else (ri <= ci), 1.0, 0.0).astype(BF16)
    cum_r = _dot_exact_lhs(tri, adt_r)
    cum_c = _dot_exact_rhs(adt_c, tri_t)
    end = 0 if reverse else T - 1
    total_c = cum_c[:, end:end + 1]
    src_w_c = delta_c * jnp.exp2(total_c - cum_c)
    chunk_decay_c = jnp.exp2(total_c)

    lane_lt64 = lax.broadcasted_iota(jnp.int32, (T, LANES), 1) < SSM_HEAD_DIM
    pairs_per_group = SSM_HEADS // SSM_GROUPS // 2

    for g in range(SSM_GROUPS):
        bm = b_ref[:, g * SSM_STATE:(g + 1) * SSM_STATE]
        cm = c_ref[:, g * SSM_STATE:(g + 1) * SSM_STATE]
        cb = lax.dot_general(cm, bm, (((1,), (1,)), ((), ())), preferred_element_type=F32)
        bt = jnp.transpose(bm.astype(F32))
        cm32 = cm.astype(F32)
        for kk in range(pairs_per_group):
            k = g * pairs_per_group + kk
            lanes = slice(k * LANES, (k + 1) * LANES)
            cols = slice(kk * LANES, (kk + 1) * LANES)
            xs = xs_ref[:, lanes]
            h_in = h_ref[g, :, cols]
            rhs = jnp.concatenate([xs, h_in.astype(BF16)], axis=0)
            ys, sts = [], []
            for hh in (2 * k, 2 * k + 1):
                a_l = jnp.broadcast_to(cum_r[:, d0 + hh:d0 + hh + 1], (T, T))
                decay = jnp.exp2(jnp.where(causal, a_l - cum_c[hh:hh + 1, :], NEG_INF))
                gmat = cb * decay * delta_c[hh:hh + 1, :]
                lhs = jnp.concatenate([gmat, cm32 * jnp.exp2(a_l)], axis=1).astype(BF16)
                ys.append(jnp.dot(lhs, rhs, preferred_element_type=F32))
                bts = (bt * src_w_c[hh:hh + 1, :]).astype(BF16)
                sts.append(jnp.dot(bts, xs, preferred_element_type=F32))
            y_ref[:, lanes] = jnp.where(lane_lt64, ys[0], ys[1]).astype(y_ref.dtype)
            cd = jnp.where(lane_lt64, chunk_decay_c[2 * k:2 * k + 1, :], chunk_decay_c[2 * k + 1:2 * k + 2, :])
            h_ref[g, :, cols] = h_in * cd + jnp.where(lane_lt64, sts[0], sts[1])


def _ssd_kernel(xsf, bf, cf, dtf, dttf, xsb, bb, cb, dtb, dttb, brow, bcol, arow, acol, yf, yb, hf, hb):
    @pl.when(pl.program_id(1) == 0)
    def _():
        hf[...] = jnp.zeros_like(hf)
        hb[...] = jnp.zeros_like(hb)

    _ssd_direction(xsf, bf, cf, dtf, dttf, brow, bcol, arow, acol, yf, hf, False)
    _ssd_direction(xsb, bb, cb, dtb, dttb, brow, bcol, arow, acol, yb, hb, True)


def _ssd(xbc, dt, dtt, dt_bias, a_log, n_seq, seq):
    m = xbc.shape[0]
    T = SSM_CHUNK
    nc = seq // T
    fwd = lambda s, i: s * nc + i
    bwd = lambda s, i: s * nc + (nc - 1 - i)
    gn = SSM_GROUPS * SSM_STATE

    def specs(row):
        return [
            pl.BlockSpec((T, SSM_WIDTH), lambda s, i: (row(s, i), 0)),
            pl.BlockSpec((T, gn), lambda s, i: (row(s, i), SSM_WIDTH // gn)),
            pl.BlockSpec((T, gn), lambda s, i: (row(s, i), SSM_WIDTH // gn + 1)),
            pl.BlockSpec((T, LANES), lambda s, i: (row(s, i), 0)),
            pl.BlockSpec((2 * SSM_HEADS, T), lambda s, i: (0, row(s, i))),
        ]

    small = lambda shape: pl.BlockSpec(shape, lambda s, i: (0, 0))
    nh2 = 2 * SSM_HEADS
    lane_row = lambda v: jnp.pad(v.reshape(1, nh2), ((0, 0), (0, LANES - nh2)))
    return pl.pallas_call(
        _ssd_kernel,
        grid=(n_seq, nc),
        in_specs=specs(fwd) + specs(bwd) + [small((1, LANES)), small((nh2, 1)), small((1, LANES)), small((nh2, 1))],
        out_specs=[
            pl.BlockSpec((T, SSM_WIDTH), lambda s, i: (fwd(s, i), 0)),
            pl.BlockSpec((T, SSM_WIDTH), lambda s, i: (bwd(s, i), 0)),
        ],
        out_shape=[jax.ShapeDtypeStruct((m, SSM_WIDTH), BF16)] * 2,
        scratch_shapes=[pltpu.VMEM((SSM_GROUPS, SSM_STATE, SSM_WIDTH // SSM_GROUPS), F32)] * 2,
        compiler_params=_cparams(("parallel", "arbitrary")),
        name="ssd",
    )(xbc, xbc, xbc, dt, dtt, xbc, xbc, xbc, dt, dtt,
      lane_row(dt_bias), dt_bias.reshape(nh2, 1), lane_row(a_log), a_log.reshape(nh2, 1))


ROW_CHUNK = 128
RESIDENT = pl.Buffered(1)


def _mix_out_kernel(att_ref, yf_ref, yb_ref, xs_ref, z_ref, dsk_ref, gn_ref, wa_ref, ws_ref, x_ref, o_ref):
    gw = SSM_WIDTH // SSM_GROUPS
    for c in range(x_ref.shape[0] // ROW_CHUNK):
        rows = slice(c * ROW_CHUNK, (c + 1) * ROW_CHUNK)
        y = ((yf_ref[rows, :].astype(F32) + yb_ref[rows, :].astype(F32) + xs_ref[rows, :].astype(F32) * dsk_ref[...])
             * _silu(z_ref[rows, :].astype(F32)))
        ssm = jnp.concatenate([_rms_rows(y[:, g * gw:(g + 1) * gw], gn_ref[:, g * gw:(g + 1) * gw])
                               for g in range(SSM_GROUPS)], axis=1).astype(BF16)
        o_ref[rows, :] = (x_ref[rows, :]
                          + jnp.dot(att_ref[rows, :], wa_ref[...], preferred_element_type=F32)
                          + jnp.dot(ssm, ws_ref[...], preferred_element_type=F32))


def _mix_out(att, yf, yb, xbc, zx, dskip, gnorm, w_out, li, x, *, tm=512):
    m = x.shape[0]
    row = pl.BlockSpec((tm, SSM_WIDTH), lambda i: (i, 0))
    vec = pl.BlockSpec((1, SSM_WIDTH), lambda i: (0, 0))
    wsp = lambda half: pl.BlockSpec((None, SSM_WIDTH, D_MODEL), lambda i: (li, half, 0),
                                    pipeline_mode=RESIDENT)
    full = pl.BlockSpec((tm, D_MODEL), lambda i: (i, 0))
    return pl.pallas_call(
        _mix_out_kernel,
        grid=(m // tm,),
        in_specs=[row, row, row, row, row, vec, vec, wsp(0), wsp(1), full],
        out_specs=full,
        out_shape=jax.ShapeDtypeStruct((m, D_MODEL), F32),
        compiler_params=_cparams(("parallel",)),
        name="mix_out",
    )(att, yf, yb, xbc, zx, dskip, gnorm, w_out, w_out, x)


def _pw1_glu_kernel(x_ref, g_ref, w_ref, b_ref, o_ref):
    for c in range(x_ref.shape[0] // ROW_CHUNK):
        rows = slice(c * ROW_CHUNK, (c + 1) * ROW_CHUNK)
        xn = _rms_rows(x_ref[rows, :], g_ref[...]).astype(BF16)
        a = jnp.dot(xn, w_ref[:, :D_MODEL], preferred_element_type=F32) + b_ref[:, :D_MODEL]
        gate = jnp.dot(xn, w_ref[:, D_MODEL:], preferred_element_type=F32) + b_ref[:, D_MODEL:]
        o_ref[rows, :] = (a * _sigmoid(gate)).astype(o_ref.dtype)


def _pw1_glu(x, g, w, li, b, *, tm=512):
    m = x.shape[0]
    full = pl.BlockSpec((tm, D_MODEL), lambda i: (i, 0))
    return pl.pallas_call(
        _pw1_glu_kernel,
        grid=(m // tm,),
        in_specs=[
            full,
            pl.BlockSpec((1, D_MODEL), lambda i: (0, 0)),
            pl.BlockSpec((None, D_MODEL, 2 * D_MODEL), lambda i: (li, 0, 0), pipeline_mode=RESIDENT),
            pl.BlockSpec((1, 2 * D_MODEL), lambda i: (0, 0)),
        ],
        out_specs=full,
        out_shape=jax.ShapeDtypeStruct((m, D_MODEL), F32),
        compiler_params=_cparams(("parallel",)),
        name="pw1_glu",
    )(x, g, w, b)


def _pw2_kernel(u_ref, lg_ref, lb_ref, w_ref, b_ref, x_ref, o_ref):
    for c in range(x_ref.shape[0] // ROW_CHUNK):
        rows = slice(c * ROW_CHUNK, (c + 1) * ROW_CHUNK)
        u = u_ref[rows, :]
        mu = jnp.mean(u, axis=-1, keepdims=True)
        uc = u - mu
        var = jnp.mean(uc * uc, axis=-1, keepdims=True)
        un = _silu(uc * lax.rsqrt(var + EPS) * lg_ref[...] + lb_ref[...]).astype(BF16)
        o_ref[rows, :] = x_ref[rows, :] + b_ref[...] + jnp.dot(un, w_ref[...], preferred_element_type=F32)


def _pw2(u, ln_g, ln_b, w, li, b, x, *, tm=512):
    m = x.shape[0]
    vec = pl.BlockSpec((1, D_MODEL), lambda i: (0, 0))
    full = pl.BlockSpec((tm, D_MODEL), lambda i: (i, 0))
    return pl.pallas_call(
        _pw2_kernel,
        grid=(m // tm,),
        in_specs=[full, vec, vec,
                  pl.BlockSpec((None, D_MODEL, D_MODEL), lambda i: (li, 0, 0), pipeline_mode=RESIDENT), vec, full],
        out_specs=full,
        out_shape=jax.ShapeDtypeStruct((m, D_MODEL), F32),
        compiler_params=_cparams(("parallel",)),
        name="pw2",
    )(u, ln_g, ln_b, w, b, x)


def _ffn_kernel(x_ref, g_ref, wg_ref, wu_ref, wd_ref, o_ref, xn_ref):
    @pl.when(pl.program_id(1) == 0)
    def _():
        x = x_ref[...]
        xn_ref[...] = _rms_rows(x, g_ref[...]).astype(BF16)
        o_ref[...] = x

    xn = xn_ref[...]
    gate = jnp.dot(xn, wg_ref[...], preferred_element_type=F32)
    up = jnp.dot(xn, wu_ref[...], preferred_element_type=F32)
    h = (_silu(gate) * up).astype(BF16)
    o_ref[...] += jnp.dot(h, wd_ref[...], preferred_element_type=F32)


def _ffn(x, g, wg, wu, wd, li, *, tm=1024, th=512):
    m = x.shape[0]
    return pl.pallas_call(
        _ffn_kernel,
        grid=(m // tm, FFN_HIDDEN // th),
        in_specs=[
            pl.BlockSpec((tm, D_MODEL), lambda i, j: (i, 0)),
            pl.BlockSpec((1, D_MODEL), lambda i, j: (0, 0)),
            pl.BlockSpec((None, D_MODEL, th), lambda i, j: (li, 0, j)),
            pl.BlockSpec((None, D_MODEL, th), lambda i, j: (li, 0, j)),
            pl.BlockSpec((None, th, D_MODEL), lambda i, j: (li, j, 0)),
        ],
        out_specs=pl.BlockSpec((tm, D_MODEL), lambda i, j: (i, 0)),
        out_shape=jax.ShapeDtypeStruct((m, D_MODEL), F32),
        scratch_shapes=[pltpu.VMEM((tm, D_MODEL), BF16)],
        compiler_params=_cparams(("parallel", "arbitrary"), VMEM_LIMIT_FFN),
        name="ffn",
    )(x, g, wg, wu, wd)


def _rope_tables(seq):
    pos = jnp.arange(seq, dtype=F32)
    inv_freq = ROPE_THETA ** (-jnp.arange(0, ROPE_DIM, 2, dtype=F32) / ROPE_DIM)
    ang = pos[:, None] * inv_freq[None, :]
    cos, sin = jnp.cos(ang), jnp.sin(ang)
    half = ROPE_DIM // 2
    zeros = lambda n: jnp.zeros((seq, n), F32)
    keep = jnp.concatenate([cos, cos, jnp.ones((seq, HEAD_DIM - ROPE_DIM), F32)], axis=-1)
    from_prev = jnp.concatenate([zeros(half), sin, zeros(HEAD_DIM - ROPE_DIM)], axis=-1)
    from_next = jnp.concatenate([-sin, zeros(HEAD_DIM - half)], axis=-1)
    return keep, from_prev, from_next


def _bf16_weights(p):
    w_in = p['w_in'].astype(BF16)
    return dict(
        w_in=w_in, w_zx=w_in[:, :, QK_COLS + ATT_WIDTH:PROJ_COLS],
        w_dt=jnp.pad(w_in[:, :, PROJ_COLS:], ((0, 0), (0, 0), (0, LANES - 2 * SSM_HEADS))),
        w_out=p['w_out'].astype(BF16), pw1_w=p['pw1_w'].astype(BF16), pw2_w=p['pw2_w'].astype(BF16),
        w_gate=p['w_gate'].astype(BF16), w_up=p['w_up'].astype(BF16), w_down=p['w_down'].astype(BF16))


def _trunk(x3, p, w):
    n_seq, seq, _ = x3.shape
    x = x3.reshape(n_seq * seq, D_MODEL)
    rope = _rope_tables(seq)
    row = lambda v: v.reshape(1, -1)
    for layer in range(DEPTH):
        if layer % 2 == 0:
            e = layer // 2
            head_gain = jnp.concatenate([jnp.tile(p['q_norm'][e] * (HEAD_DIM ** -0.5 * math.log2(math.e)), ATT_HEADS),
                                         jnp.tile(p['k_norm'][e], ATT_HEADS)]).reshape(1, QK_COLS)
            qk, v, zx, dt = _in_proj(x, row(p['mix_norm'][e]), w['w_in'], w['w_zx'], w['w_dt'], e,
                                     head_gain, rope, seq)
            att = _attention(qk, v, n_seq, seq)
            xbc = _dwconv(zx, p['ssm_conv_w'][e], p['ssm_conv_b'][e], n_seq, seq, SSM_WIDTH, SSM_CONV_CH, True, BF16, 512)
            dtt = jnp.transpose(dt[:, :2 * SSM_HEADS])
            yf, yb = _ssd(xbc, dt, dtt, p['dt_bias'][e], p['a_log'][e], n_seq, seq)
            dskip = jnp.repeat(p['d_skip'][e], SSM_HEAD_DIM).reshape(1, SSM_WIDTH)
            x = _mix_out(att, yf, yb, xbc, zx, dskip, row(p['ssm_norm'][e]), w['w_out'], e, x)
        else:
            o = layer // 2
            u = _pw1_glu(x, row(p['conf_norm'][o]), w['pw1_w'], o, row(p['pw1_b'][o]))
            u = _dwconv(u, p['dw_w'][o], p['dw_b'][o], n_seq, seq, 0, D_MODEL, False, F32, 256)
            x = _pw2(u, row(p['ln_g'][o]), row(p['ln_b'][o]), w['pw2_w'], o, row(p['pw2_b'][o]), x)
        x = _ffn(x, row(p['ffn_norm'][layer]), w['w_gate'], w['w_up'], w['w_down'], layer)
    return x.reshape(n_seq, seq, D_MODEL)


def kernel(x_prompt, x_sample, mix_norm, w_in, q_norm, k_norm, ssm_conv_w, ssm_conv_b, a_log, dt_bias, d_skip, ssm_norm, w_out, conf_norm, pw1_w, pw1_b, dw_w, dw_b, ln_g, ln_b, pw2_w, pw2_b, ffn_norm, w_gate, w_up, w_down):
    params = dict(mix_norm=mix_norm, w_in=w_in, q_norm=q_norm, k_norm=k_norm, ssm_conv_w=ssm_conv_w,
                  ssm_conv_b=ssm_conv_b, a_log=a_log, dt_bias=dt_bias, d_skip=d_skip, ssm_norm=ssm_norm,
                  w_out=w_out, conf_norm=conf_norm, pw1_w=pw1_w, pw1_b=pw1_b, dw_w=dw_w, dw_b=dw_b,
                  ln_g=ln_g, ln_b=ln_b, pw2_w=pw2_w, pw2_b=pw2_b, ffn_norm=ffn_norm, w_gate=w_gate,
                  w_up=w_up, w_down=w_down)
    weights = _bf16_weights(params)
    return (_trunk(x_prompt, params, weights), _trunk(x_sample, params, weights))
```
